```python
import math
import jax, jax.numpy as jnp
from jax import lax
import numpy as np

D_MODEL = 1024
BATCH = 2
SEQ = 8192
DEPTH = 2
DEC_BATCH = 128
DEC_SEQ = 8
PAST_LEN = 16384
PAGE_SIZE = 128

HEAD_DIM = 64
GM_WIDTH = D_MODEL // 2
GM_GROUPS = 4
GM_GROUP_CH = GM_WIDTH // GM_GROUPS
GM_CHUNK = 128
RET_HEADS = 8
RET_DK = 64
RET_DV = 64
RET_CHUNK = 128
ATT_Q_HEADS = 8
ATT_KV_HEADS = 2
ATT_REP = ATT_Q_HEADS // ATT_KV_HEADS
WINDOW = 128
ROPE_THETA = 10000.0
N_BRANCH = 3
PEER_HEADS = 8
PEER_NKEYS = 128
PEER_EXPERTS = PEER_NKEYS * PEER_NKEYS
PEER_DQ = 256
PEER_TOPK = 16
PEER_BLOCK = 256
EPS = 1e-6
NEG = -1e30

IN_WIDTHS = (GM_WIDTH, GM_WIDTH,
             RET_HEADS * RET_DK, RET_HEADS * RET_DK, RET_HEADS * RET_DV, RET_HEADS * RET_DV,
             ATT_Q_HEADS * HEAD_DIM, ATT_KV_HEADS * HEAD_DIM, ATT_KV_HEADS * HEAD_DIM,
             N_BRANCH * D_MODEL)
D_IN = sum(IN_WIDTHS)
SPLIT_POINTS = tuple(int(s) for s in np.cumsum(IN_WIDTHS)[:-1])

kernel_name = 'hybrid_gmlp_retention_swa_peer_step'


def rmsnorm(x, w):
    xf = x.astype(jnp.float32)
    y = xf * lax.rsqrt(jnp.mean(xf * xf, axis=-1, keepdims=True) + EPS)
    return (y * w.astype(jnp.float32)).astype(x.dtype)


def rope(x, pos):
    half = HEAD_DIM // 2
    inv = jnp.exp(-math.log(ROPE_THETA) * jnp.arange(half, dtype=jnp.float32) / half)
    ang = pos.astype(jnp.float32)[:, None] * inv[None, :]
    cos = jnp.cos(ang)[:, None, :]
    sin = jnp.sin(ang)[:, None, :]
    xf = x.astype(jnp.float32)
    x1, x2 = xf[..., :half], xf[..., half:]
    return jnp.concatenate([x1 * cos - x2 * sin, x2 * cos + x1 * sin], axis=-1).astype(x.dtype)


def gmlp_branch(gu, gv, ln_w, ln_b, ws, bs):
    N, T, _ = gu.shape
    u = jax.nn.gelu(gu, approximate=False)
    vf = jax.nn.gelu(gv, approximate=False).astype(jnp.float32)
    mu = jnp.mean(vf, axis=-1, keepdims=True)
    var = jnp.mean(jnp.square(vf - mu), axis=-1, keepdims=True)
    vn = ((vf - mu) * lax.rsqrt(var + EPS) * ln_w.astype(jnp.float32)
          + ln_b.astype(jnp.float32)).astype(gu.dtype)
    ch = min(T, GM_CHUNK)
    vc = vn.reshape(N, T // ch, ch, GM_GROUPS, GM_GROUP_CH)
    causal = jnp.tril(jnp.ones((ch, ch), dtype=bool))
    wm = jnp.where(causal[None], ws[:, :ch, :ch], jnp.zeros((), ws.dtype))
    s = jnp.einsum('gts,bnsgc->bntgc', wm, vc) + jnp.transpose(bs[:, :ch])[:, :, None]
    return u * s.reshape(N, T, GM_WIDTH), vn


def ret_log_decay():
    return jnp.log1p(-jnp.exp2(-5.0 - jnp.arange(RET_HEADS, dtype=jnp.float32)))


def retention_chunk(S, qkv):
    q, k, v = qkv
    C = q.shape[2]
    lg = ret_log_decay()[:, None]
    idx = jnp.arange(C, dtype=jnp.float32)
    diff = idx[:, None] - idx[None, :]
    dmat = jnp.where(diff[None] >= 0, jnp.exp(jnp.maximum(diff, 0.0)[None] * lg[:, :, None]), 0.0)
    scores = jnp.einsum('nhid,nhjd->nhij', q, k) * dmat.astype(q.dtype)
    o = jnp.einsum('nhij,nhje->nhie', scores, v)
    q_dec = jnp.exp((idx + 1.0)[None] * lg).astype(q.dtype)
    o = o + jnp.einsum('nhid,nhde->nhie', q * q_dec[:, :, None], S.astype(q.dtype))
    k_dec = jnp.exp((C - 1.0 - idx)[None] * lg).astype(k.dtype)
    S_new = (S * jnp.exp(C * lg)[:, :, None].astype(S.dtype)
             + jnp.einsum('nhjd,nhje->nhde', k * k_dec[:, :, None], v).astype(S.dtype))
    return S_new.astype(S.dtype), o


def retention_branch(rq, rk, rv, rg, gn_w, S0, pos):
    N, T, _ = rq.shape
    q = rope(rq.reshape(N, T, RET_HEADS, RET_DK), pos)
    k = rope(rk.reshape(N, T, RET_HEADS, RET_DK), pos) * (RET_DK ** -0.5)
    v = rv.reshape(N, T, RET_HEADS, RET_DV)
    c = min(T, RET_CHUNK)
    nc = T // c

    def to_chunks(a):
        return jnp.transpose(a.reshape(N, nc, c, RET_HEADS, a.shape[-1]), (1, 0, 3, 2, 4))

    S_fin, o = lax.scan(retention_chunk, S0, (to_chunks(q), to_chunks(k), to_chunks(v)))
    o = jnp.transpose(o, (1, 0, 3, 2, 4)).reshape(N, T, RET_HEADS, RET_DV)
    of = o.astype(jnp.float32)
    mu = jnp.mean(of, axis=-1, keepdims=True)
    var = jnp.mean(jnp.square(of - mu), axis=-1, keepdims=True)
    y = ((of - mu) * lax.rsqrt(var + EPS)).reshape(N, T, RET_HEADS * RET_DV) * gn_w.astype(jnp.float32)
    return jax.nn.silu(rg) * y.astype(rg.dtype), S_fin


def sink_softmax(s, valid, sinks):
    s = jnp.where(valid, s, NEG)
    sk = jnp.broadcast_to(sinks.astype(jnp.float32).reshape(ATT_KV_HEADS, ATT_REP, 1, 1),
                          s.shape[:-1] + (1,))
    p = jax.nn.softmax(jnp.concatenate([s, sk], axis=-1), axis=-1)
    return p[..., :-1]


def swa_prompt(q, k, v, sinks):
    B, S = q.shape[0], q.shape[1]
    nb = S // WINDOW
    qb = q.reshape(B, nb, WINDOW, ATT_KV_HEADS, ATT_REP, HEAD_DIM)

    def band(a):
        ab = a.reshape(B, nb, WINDOW, ATT_KV_HEADS, HEAD_DIM)
        prev = jnp.concatenate([jnp.zeros_like(ab[:, :1]), ab[:, :-1]], axis=1)
        return jnp.concatenate([prev, ab], axis=2)

    kb, vb = band(k), band(v)
    s = jnp.einsum('bnqgrd,bnkgd->bngrqk', qb, kb).astype(jnp.float32) * (HEAD_DIM ** -0.5)
    blk = jnp.arange(nb)[:, None]
    qpos = blk * WINDOW + jnp.arange(WINDOW)[None]
    kpos = (blk - 1) * WINDOW + jnp.arange(2 * WINDOW)[None]
    diff = qpos[:, :, None] - kpos[:, None, :]
    valid = (kpos[:, None, :] >= 0) & (diff >= 0) & (diff < WINDOW)
    p = sink_softmax(s, valid[None, :, None, None], sinks)
    o = jnp.einsum('bngrqk,bnkgd->bnqgrd', p.astype(v.dtype), vb)
    return o.reshape(B, S, ATT_Q_HEADS * HEAD_DIM)


def swa_sample(q, k, v, win_k, win_v, sinks, pos0):
    N, T = q.shape[0], q.shape[1]
    wb = win_k.shape[1]
    kall = jnp.concatenate([win_k, k], axis=1)
    vall = jnp.concatenate([win_v, v], axis=1)
    qg = q.reshape(N, T, ATT_KV_HEADS, ATT_REP, HEAD_DIM)
    s = jnp.einsum('ntgrd,nkgd->ngrtk', qg, kall).astype(jnp.float32) * (HEAD_DIM ** -0.5)
    qpos = pos0 + jnp.arange(T)
    kpos = pos0 - wb + jnp.arange(wb + T)
    diff = qpos[:, None] - kpos[None, :]
    valid = (diff >= 0) & (diff < WINDOW)
    p = sink_softmax(s, valid, sinks)
    o = jnp.einsum('ngrtk,nkgd->ntgrd', p.astype(vall.dtype), vall)
    return o.reshape(N, T, ATT_Q_HEADS * HEAD_DIM), kall[:, -wb:], vall[:, -wb:]


def token_mixer(xn, pos0, S0, win_k, win_v, lp):
    N, T, _ = xn.shape
    pos = pos0 + jnp.arange(T, dtype=jnp.int32)
    gu, gv, rq, rk, rv, rg, aq, ak, av, gl = jnp.split(xn @ lp['w_in'], SPLIT_POINTS, axis=-1)
    o_a, vn = gmlp_branch(gu, gv, lp['gm_ln_w'], lp['gm_ln_b'], lp['gm_ws'], lp['gm_b'])
    o_b, S_fin = retention_branch(rq, rk, rv, rg, lp['ret_gn_w'], S0, pos)
    q = rope(aq.reshape(N, T, ATT_Q_HEADS, HEAD_DIM), pos)
    k = rope(ak.reshape(N, T, ATT_KV_HEADS, HEAD_DIM), pos)
    v = av.reshape(N, T, ATT_KV_HEADS, HEAD_DIM)
    if win_k is None:
        o_c = swa_prompt(q, k, v, lp['sinks'])
        keep = min(WINDOW, T)
        new_k, new_v = k[:, T - keep:], v[:, T - keep:]
    else:
        o_c, new_k, new_v = swa_sample(q, k, v, win_k, win_v, lp['sinks'], pos0)
    ga, gb, gc = jnp.split(jax.nn.sigmoid(gl + lp['b_gate']), N_BRANCH, axis=-1)
    m = ga * (o_a @ lp['w_br_a']) + gb * (o_b @ lp['w_br_b']) + gc * (o_c @ lp['w_br_c'])
    return m @ lp['w_out'], S_fin, new_k, new_v, vn


def peer_block(xb, wq, k1, k2, pu, pv):
    M = xb.shape[0]
    half = PEER_DQ // 2
    q = (xb @ wq).reshape(M, PEER_HEADS, PEER_DQ)
    s1 = jnp.einsum('mhd,kd->mhk', q[..., :half], k1).astype(jnp.float32)
    s2 = jnp.einsum('mhd,kd->mhk', q[..., half:], k2).astype(jnp.float32)
    v1, i1 = lax.top_k(s1, PEER_TOPK)
    v2, i2 = lax.top_k(s2, PEER_TOPK)
    ncand = PEER_TOPK * PEER_TOPK
    cand = (v1[..., :, None] + v2[..., None, :]).reshape(M, PEER_HEADS, ncand)
    cidx = (i1[..., :, None] * PEER_NKEYS + i2[..., None, :]).reshape(M, PEER_HEADS, ncand)
    sc, sel = lax.top_k(cand, PEER_TOPK)
    eidx = jnp.take_along_axis(cidx, sel, axis=-1)
    g = jax.nn.softmax(sc, axis=-1).astype(xb.dtype)
    ug = jnp.take(pu, eidx, axis=0)
    vg = jnp.take(pv, eidx, axis=0)
    act = jax.nn.gelu(jnp.einsum('md,mhkd->mhk', xb, ug), approximate=False)
    return jnp.einsum('mhk,mhkd->md', g * act, vg)


def peer(xn, lp):
    shp = xn.shape
    flat = xn.reshape(-1, D_MODEL)
    n = flat.shape[0]
    nb = -(-n // PEER_BLOCK)
    flat = jnp.pad(flat, ((0, nb * PEER_BLOCK - n), (0, 0)))
    out = lax.map(lambda xb: peer_block(xb, lp['peer_wq'], lp['peer_k1'], lp['peer_k2'],
                                        lp['peer_u'], lp['peer_v']),
                  flat.reshape(nb, PEER_BLOCK, D_MODEL))
    return out.reshape(nb * PEER_BLOCK, D_MODEL)[:n].reshape(shp)


def trunk(x, pos0, S_in, k_in, v_in, layers, normf_w):
    Ss, ks, vs, gs = [], [], [], []
    for l in range(DEPTH):
        lp = layers[l]
        if S_in is None:
            S0 = jnp.zeros((x.shape[0], RET_HEADS, RET_DK, RET_DV), x.dtype)
            wk, wv = None, None
        else:
            S0, wk, wv = S_in[l], k_in[l], v_in[l]
        mix, S_fin, nk, nv, vn = token_mixer(rmsnorm(x, lp['n1']), pos0, S0, wk, wv, lp)
        x = x + mix
        x = x + peer(rmsnorm(x, lp['n2']), lp)
        Ss.append(S_fin)
        ks.append(nk)
        vs.append(nv)
        gs.append(vn)
    return rmsnorm(x, normf_w), Ss, ks, vs, gs


def setup_inputs(seed: int = 0) -> dict:
    key = jax.random.key(seed)
    ks = jax.random.split(key, 25)
    nrm = jax.random.normal
    f32 = jnp.float32
    wb = min(WINDOW, PAST_LEN)
    half = PEER_DQ // 2
    return {
        'x_prompt': nrm(ks[0], (BATCH, SEQ, D_MODEL), f32),
        'x_sample': nrm(ks[1], (DEC_BATCH, DEC_SEQ, D_MODEL), f32),
        'state_ret': 0.5 * nrm(ks[2], (DEPTH, DEC_BATCH, RET_HEADS, RET_DK, RET_DV), f32),
        'cache_win_k': nrm(ks[3], (DEPTH, DEC_BATCH, wb, ATT_KV_HEADS, HEAD_DIM), f32),
        'cache_win_v': nrm(ks[4], (DEPTH, DEC_BATCH, wb, ATT_KV_HEADS, HEAD_DIM), f32),
        'norm1_w': 1.0 + 0.02 * nrm(ks[5], (DEPTH, D_MODEL), f32),
        'norm2_w': 1.0 + 0.02 * nrm(ks[6], (DEPTH, D_MODEL), f32),
        'normf_w': 1.0 + 0.02 * nrm(ks[7], (D_MODEL,), f32),
        'w_in': nrm(ks[8], (DEPTH, D_MODEL, D_IN), f32) * D_MODEL ** -0.5,
        'b_gate': 0.02 * nrm(ks[9], (DEPTH, N_BRANCH * D_MODEL), f32),
        'gm_ln_w': 1.0 + 0.02 * nrm(ks[10], (DEPTH, GM_WIDTH), f32),
        'gm_ln_b': 0.02 * nrm(ks[11], (DEPTH, GM_WIDTH), f32),
        'gm_ws': nrm(ks[12], (DEPTH, GM_GROUPS, GM_CHUNK, GM_CHUNK), f32) * GM_CHUNK ** -0.5,
        'gm_b': 1.0 + 0.02 * nrm(ks[13], (DEPTH, GM_GROUPS, GM_CHUNK), f32),
        'ret_gn_w': 1.0 + 0.02 * nrm(ks[14], (DEPTH, RET_HEADS * RET_DV), f32),
        'attn_sinks': nrm(ks[15], (DEPTH, ATT_Q_HEADS), f32),
        'w_br_a': nrm(ks[16], (DEPTH, GM_WIDTH, D_MODEL), f32) * GM_WIDTH ** -0.5,
        'w_br_b': nrm(ks[17], (DEPTH, RET_HEADS * RET_DV, D_MODEL), f32) * (RET_HEADS * RET_DV) ** -0.5,
        'w_br_c': nrm(ks[18], (DEPTH, ATT_Q_HEADS * HEAD_DIM, D_MODEL), f32) * (ATT_Q_HEADS * HEAD_DIM) ** -0.5,
        'w_out': nrm(ks[19], (DEPTH, D_MODEL, D_MODEL), f32) * D_MODEL ** -0.5,
        'peer_wq': nrm(ks[20], (DEPTH, D_MODEL, PEER_HEADS * PEER_DQ), f32) * D_MODEL ** -0.5,
        'peer_k1': nrm(ks[21], (DEPTH, PEER_NKEYS, half), f32) * half ** -0.5,
        'peer_k2': nrm(ks[22], (DEPTH, PEER_NKEYS, half), f32) * half ** -0.5,
        'peer_u': nrm(ks[23], (DEPTH, PEER_EXPERTS, D_MODEL), f32) * D_MODEL ** -0.5,
        'peer_v': nrm(ks[24], (DEPTH, PEER_EXPERTS, D_MODEL), f32) * PEER_HEADS ** -0.5,
    }


def reference(x_prompt, x_sample, state_ret, cache_win_k, cache_win_v, norm1_w, norm2_w, normf_w,
              w_in, b_gate, gm_ln_w, gm_ln_b, gm_ws, gm_b, ret_gn_w, attn_sinks,
              w_br_a, w_br_b, w_br_c, w_out, peer_wq, peer_k1, peer_k2, peer_u, peer_v):
    layers = [dict(n1=norm1_w[l], n2=norm2_w[l], w_in=w_in[l], b_gate=b_gate[l],
                   gm_ln_w=gm_ln_w[l], gm_ln_b=gm_ln_b[l], gm_ws=gm_ws[l], gm_b=gm_b[l],
                   ret_gn_w=ret_gn_w[l], sinks=attn_sinks[l],
                   w_br_a=w_br_a[l], w_br_b=w_br_b[l], w_br_c=w_br_c[l], w_out=w_out[l],
                   peer_wq=peer_wq[l], peer_k1=peer_k1[l], peer_k2=peer_k2[l],
                   peer_u=peer_u[l], peer_v=peer_v[l])
              for l in range(DEPTH)]
    y_prompt, sp, kp, vp, _ = trunk(x_prompt, 0, None, None, None, layers, normf_w)
    y_sample, ss, ksm, vsm, gsm = trunk(x_sample, PAST_LEN, state_ret, cache_win_k, cache_win_v,
                                        layers, normf_w)
    return (y_prompt, y_sample, jnp.stack(sp), jnp.stack(ss), jnp.stack(kp), jnp.stack(vp),
            jnp.stack(ksm), jnp.stack(vsm), jnp.stack(gsm))
```

```python
import functools
import math

import numpy as np
import jax
import jax.numpy as jnp
from jax import lax
from jax.experimental import pallas as pl
from jax.experimental.pallas import tpu as pltpu

F32 = jnp.float32
BF16 = jnp.bfloat16

D_MODEL = 1024
HEAD_DIM = 64
CHUNK = 128
GM_WIDTH = 512
GM_GROUPS = 4
RET_HEADS = 8
ATT_Q_HEADS = 8
ATT_KV_HEADS = 2
ATT_REP = 4
ROPE_THETA = 10000.0
PAST_LEN = 16384
PEER_HEADS = 8
PEER_NKEYS = 128
PEER_TOPK = 16
EPS = 1e-6
NEG = -1e30
MIX_WIDTH = 3840
GATE_WIDTH = 3 * D_MODEL
SEQ_PER_STEP = 16
LANES = 128
PEER_EXPERT_BLOCK = 1024
VMEM_LIMIT = 56 * 1024 * 1024


def _tile(m, target):
    t = min(m, target)
    while m % t or t % LANES:
        t -= LANES
    return t


def _rms(x, w):
    return x * lax.rsqrt(jnp.mean(x * x, axis=-1, keepdims=True) + EPS) * w


def _gelu(x):
    return 0.5 * x * (1.0 + lax.erf(x * np.float32(math.sqrt(0.5))))


def _rope(x, cos, sin_signed):
    w = x.shape[-1]
    lane = lax.broadcasted_iota(jnp.int32, x.shape, 1)
    first = (lane & (HEAD_DIM - 1)) < HEAD_DIM // 2
    rot = jnp.where(first, pltpu.roll(x, w - HEAD_DIM // 2, 1), pltpu.roll(x, HEAD_DIM // 2, 1))
    return x * cos + rot * sin_signed


def _head(x, h):
    return x[:, h * HEAD_DIM:(h + 1) * HEAD_DIM]


def _gmlp(gu, gv, lnw, lnb, wm_ref, gb_ref):
    u = _gelu(gu)
    vf = _gelu(gv)
    mu = jnp.mean(vf, axis=-1, keepdims=True)
    var = jnp.mean(jnp.square(vf - mu), axis=-1, keepdims=True)
    vn = (vf - mu) * lax.rsqrt(var + EPS) * lnw + lnb
    vnb = vn.astype(BF16)
    parts = []
    for g in range(GM_GROUPS):
        s = jnp.dot(wm_ref[g], vnb[:, g * LANES:(g + 1) * LANES], preferred_element_type=F32)
        parts.append(s + gb_ref[g])
    return u * jnp.concatenate(parts, axis=-1), vn


def _group_norm(o):
    mu = jnp.mean(o, axis=-1, keepdims=True)
    var = jnp.mean(jnp.square(o - mu), axis=-1, keepdims=True)
    return (o - mu) * lax.rsqrt(var + EPS)


def _dot_nt(a, b):
    return lax.dot_general(a, b, (((1,), (1,)), ((), ())), preferred_element_type=F32)


def _in_proj_kernel(x_ref, nw_ref, w_ref, o_ref):
    xn = _rms(x_ref[...], nw_ref[...]).astype(BF16)
    o_ref[...] = jnp.dot(xn, w_ref[...], preferred_element_type=F32)


def _in_proj(x, nw, w):
    m, d = x.shape
    n = w.shape[1]
    tm = _tile(m, 512)
    return pl.pallas_call(
        _in_proj_kernel,
        grid=(m // tm,),
        in_specs=[pl.BlockSpec((tm, d), lambda i: (i, 0)),
                  pl.BlockSpec((1, d), lambda i: (0, 0)),
                  pl.BlockSpec((d, n), lambda i: (0, 0))],
        out_specs=pl.BlockSpec((tm, n), lambda i: (i, 0)),
        out_shape=jax.ShapeDtypeStruct((m, n), F32),
        compiler_params=pltpu.CompilerParams(dimension_semantics=("parallel",),
                                             vmem_limit_bytes=VMEM_LIMIT),
        name="in_proj",
    )(x, nw, w)


def _prompt_mixer_kernel(proj_ref, cos_ref, sin_ref, wm_ref, gb_ref, lnw_ref, lnb_ref,
                         dmat_ref, qdec_ref, kdec_ref, sdec_ref, gnw_ref, sinks_ref,
                         ocat_ref, sfin_ref, kout_ref, vout_ref,
                         s_scr, kprev_scr, vprev_scr):
    c = pl.program_id(1)

    @pl.when(c == 0)
    def _reset():
        s_scr[...] = jnp.zeros_like(s_scr)
        kprev_scr[...] = jnp.zeros_like(kprev_scr)
        vprev_scr[...] = jnp.zeros_like(vprev_scr)

    cos = cos_ref[...]
    sin = sin_ref[...]

    o_a, _ = _gmlp(proj_ref[:, 0:512], proj_ref[:, 512:1024], lnw_ref[...], lnb_ref[...],
                   wm_ref, gb_ref)
    ocat_ref[:, 0:512] = o_a.astype(BF16)

    q = _rope(proj_ref[:, 1024:1536], cos, sin)
    ks = _rope(proj_ref[:, 1536:2048], cos, sin) * (HEAD_DIM ** -0.5)
    v = proj_ref[:, 2048:2560]
    qd = q * qdec_ref[...]
    kd = ks * kdec_ref[...]
    ys = []
    for h in range(RET_HEADS):
        qh = _head(q, h).astype(BF16)
        kh = _head(ks, h).astype(BF16)
        vh = _head(v, h).astype(BF16)
        sc = _dot_nt(qh, kh) * dmat_ref[h]
        o = jnp.dot(sc.astype(BF16), vh, preferred_element_type=F32)
        s_old = s_scr[h]
        o = o + jnp.dot(_head(qd, h).astype(BF16), s_old.astype(BF16), preferred_element_type=F32)
        kdh = _head(kd, h).astype(BF16)
        s_new = s_old * sdec_ref[h] + lax.dot_general(
            kdh, vh, (((0,), (0,)), ((), ())), preferred_element_type=F32)
        s_scr[h] = s_new
        sfin_ref[h] = s_new
        ys.append(_group_norm(o))
    y = jnp.concatenate(ys, axis=-1) * gnw_ref[...]
    rg = proj_ref[:, 2560:3072]
    ocat_ref[:, 512:1024] = (rg * jax.nn.sigmoid(rg) * y).astype(BF16)

    qa = _rope(proj_ref[:, 3072:3584], cos, sin)
    ka = _rope(proj_ref[:, 3584:3712], cos[:, 0:128], sin[:, 0:128])
    va = proj_ref[:, 3712:3840]
    kout_ref[...] = ka
    vout_ref[...] = va
    rows = ATT_REP * CHUNK
    tt = lax.broadcasted_iota(jnp.int32, (rows, 2 * CHUNK), 0) & (CHUNK - 1)
    kk = lax.broadcasted_iota(jnp.int32, (rows, 2 * CHUNK), 1)
    valid = (kk > tt) & (kk <= tt + CHUNK) & ((kk >= CHUNK) | (c > 0))
    rr = lax.broadcasted_iota(jnp.int32, (rows, 1), 0) // CHUNK
    outs = [None] * ATT_Q_HEADS
    for g in range(ATT_KV_HEADS):
        kcat = jnp.concatenate([_head(kprev_scr[...], g), _head(ka, g)], axis=0).astype(BF16)
        vcat = jnp.concatenate([_head(vprev_scr[...], g), _head(va, g)], axis=0).astype(BF16)
        qg = jnp.concatenate([_head(qa, g * ATT_REP + r) for r in range(ATT_REP)],
                             axis=0).astype(BF16)
        s = _dot_nt(qg, kcat) * (HEAD_DIM ** -0.5)
        s = jnp.where(valid, s, NEG)
        sink = jnp.zeros((rows, 1), F32)
        for r in range(ATT_REP):
            sink = jnp.where(rr == r, sinks_ref[g * ATT_REP + r], sink)
        mx = jnp.maximum(jnp.max(s, axis=-1, keepdims=True), sink)
        e = jnp.exp(s - mx)
        den = jnp.sum(e, axis=-1, keepdims=True) + jnp.exp(sink - mx)
        p = (e / den).astype(BF16)
        o = jnp.dot(p, vcat, preferred_element_type=F32)
        for r in range(ATT_REP):
            outs[g * ATT_REP + r] = o[r * CHUNK:(r + 1) * CHUNK]
    ocat_ref[:, 1024:1536] = jnp.concatenate(outs, axis=-1).astype(BF16)
    kprev_scr[...] = ka
    vprev_scr[...] = va


def _prompt_mixer(proj, m_total, batch, seq, tabs, lw):
    nchunk = seq // CHUNK
    const2 = lambda b, c: (0, 0)
    const3 = lambda b, c: (0, 0, 0)
    row_blk = lambda b, c: (b * nchunk + c, 0)
    return pl.pallas_call(
        _prompt_mixer_kernel,
        grid=(batch, nchunk),
        in_specs=[pl.BlockSpec((CHUNK, MIX_WIDTH), row_blk),
                  pl.BlockSpec((CHUNK, 512), lambda b, c: (c, 0)),
                  pl.BlockSpec((CHUNK, 512), lambda b, c: (c, 0)),
                  pl.BlockSpec((GM_GROUPS, CHUNK, CHUNK), const3),
                  pl.BlockSpec((GM_GROUPS, CHUNK, CHUNK), const3),
                  pl.BlockSpec((1, 512), const2),
                  pl.BlockSpec((1, 512), const2),
                  pl.BlockSpec((RET_HEADS, CHUNK, CHUNK), const3),
                  pl.BlockSpec((CHUNK, 512), const2),
                  pl.BlockSpec((CHUNK, 512), const2),
                  pl.BlockSpec((RET_HEADS, HEAD_DIM, HEAD_DIM), const3),
                  pl.BlockSpec((1, 512), const2),
                  pl.BlockSpec(memory_space=pltpu.SMEM)],
        out_specs=[pl.BlockSpec((CHUNK, 3 * 512), row_blk),
                   pl.BlockSpec((None, RET_HEADS, HEAD_DIM, HEAD_DIM), lambda b, c: (b, 0, 0, 0)),
                   pl.BlockSpec((None, CHUNK, 128), lambda b, c: (b, 0, 0)),
                   pl.BlockSpec((None, CHUNK, 128), lambda b, c: (b, 0, 0))],
        out_shape=[jax.ShapeDtypeStruct((m_total, 3 * 512), BF16),
                   jax.ShapeDtypeStruct((batch, RET_HEADS, HEAD_DIM, HEAD_DIM), F32),
                   jax.ShapeDtypeStruct((batch, CHUNK, 128), F32),
                   jax.ShapeDtypeStruct((batch, CHUNK, 128), F32)],
        scratch_shapes=[pltpu.VMEM((RET_HEADS, HEAD_DIM, HEAD_DIM), F32),
                        pltpu.VMEM((CHUNK, 128), F32),
                        pltpu.VMEM((CHUNK, 128), F32)],
        compiler_params=pltpu.CompilerParams(dimension_semantics=("parallel", "arbitrary"),
                                             vmem_limit_bytes=VMEM_LIMIT),
        name="prompt_mixer",
    )(proj, tabs["cos_p"], tabs["sin_p"], lw["wm_p"], lw["gb_p"], lw["ln_w"], lw["ln_b"],
      tabs["dmat_p"], tabs["qdec_p"], tabs["kdec_p"], tabs["sdec_p"], lw["gn_w"], lw["sinks"])


def _sample_mixer_kernel(ocat_in_ref, proj_ref, s0_ref, ck_ref, cv_ref, cos_ref, sin_ref,
                         wm_ref, gb_ref, lnw_ref, lnb_ref, dmat_ref, qdec_ref, kdec_ref,
                         sdec_ref, gnw_ref, sinks_ref,
                         ocat_ref, snew_ref, kout_ref, vout_ref, vn_ref):
    del ocat_in_ref
    nb = s0_ref.shape[0]
    t_len = proj_ref.shape[0] // nb
    cos = cos_ref[...]
    sin = sin_ref[...]

    o_a, vn = _gmlp(proj_ref[:, 0:512], proj_ref[:, 512:1024], lnw_ref[...], lnb_ref[...],
                    wm_ref, gb_ref)
    ocat_ref[:, 0:512] = o_a.astype(BF16)
    vn_ref[...] = vn

    def seq3(x):
        return x.reshape(nb, t_len, x.shape[-1])

    q = _rope(proj_ref[:, 1024:1536], cos, sin)
    ks = _rope(proj_ref[:, 1536:2048], cos, sin) * (HEAD_DIM ** -0.5)
    v = proj_ref[:, 2048:2560]
    qd = q * qdec_ref[...]
    kd = ks * kdec_ref[...]
    ys = []
    for h in range(RET_HEADS):
        qh = _head(q, h).astype(BF16)
        kh = _head(ks, h).astype(BF16)
        vh = _head(v, h).astype(BF16)
        sc = _dot_nt(qh, kh) * dmat_ref[h]
        o = jnp.dot(sc.astype(BF16), vh, preferred_element_type=F32)
        s_old = s0_ref[:, h]
        oc = jnp.einsum("ntd,nde->nte", seq3(_head(qd, h)).astype(BF16), s_old.astype(BF16),
                        preferred_element_type=F32)
        o = o + oc.reshape(nb * t_len, HEAD_DIM)
        kd3t = jnp.swapaxes(seq3(_head(kd, h)), 1, 2).astype(BF16)
        upd = jnp.einsum("ndt,nte->nde", kd3t, seq3(_head(v, h)).astype(BF16),
                         preferred_element_type=F32)
        snew_ref[:, h] = s_old * sdec_ref[h] + upd
        ys.append(_group_norm(o))
    y = jnp.concatenate(ys, axis=-1) * gnw_ref[...]
    rg = proj_ref[:, 2560:3072]
    ocat_ref[:, 512:1024] = (rg * jax.nn.sigmoid(rg) * y).astype(BF16)

    qa = _rope(proj_ref[:, 3072:3584], cos, sin)
    ka = _rope(proj_ref[:, 3584:3712], cos[:, 0:128], sin[:, 0:128])
    va = proj_ref[:, 3712:3840]
    wb = ck_ref.shape[1]
    kout_ref[:, 0:wb - t_len, :] = ck_ref[:, t_len:wb, :]
    kout_ref[:, wb - t_len:wb, :] = seq3(ka)
    vout_ref[:, 0:wb - t_len, :] = cv_ref[:, t_len:wb, :]
    vout_ref[:, wb - t_len:wb, :] = seq3(va)
    nq = ATT_REP * t_len
    tq_c = lax.broadcasted_iota(jnp.int32, (nb, nq, wb), 1) & (t_len - 1)
    kk_c = lax.broadcasted_iota(jnp.int32, (nb, nq, wb), 2)
    valid_c = kk_c > tq_c
    tq_n = lax.broadcasted_iota(jnp.int32, (nb, nq, t_len), 1) & (t_len - 1)
    kk_n = lax.broadcasted_iota(jnp.int32, (nb, nq, t_len), 2)
    valid_n = kk_n <= tq_n
    rr = lax.broadcasted_iota(jnp.int32, (1, nq, 1), 1) // t_len
    outs = [None] * ATT_Q_HEADS
    for g in range(ATT_KV_HEADS):
        gs = slice(g * HEAD_DIM, (g + 1) * HEAD_DIM)
        kc = ck_ref[:, :, gs].astype(BF16)
        vc = cv_ref[:, :, gs].astype(BF16)
        kn = seq3(_head(ka, g)).astype(BF16)
        vnw = seq3(_head(va, g)).astype(BF16)
        qg = jnp.concatenate([seq3(_head(qa, g * ATT_REP + r)) for r in range(ATT_REP)],
                             axis=1).astype(BF16)
        s_c = jnp.einsum("nqd,nkd->nqk", qg, kc, preferred_element_type=F32) * (HEAD_DIM ** -0.5)
        s_n = jnp.einsum("nqd,nkd->nqk", qg, kn, preferred_element_type=F32) * (HEAD_DIM ** -0.5)
        s_c = jnp.where(valid_c, s_c, NEG)
        s_n = jnp.where(valid_n, s_n, NEG)
        sink = jnp.zeros((1, nq, 1), F32)
        for r in range(ATT_REP):
            sink = jnp.where(rr == r, sinks_ref[g * ATT_REP + r], sink)
        mx = jnp.maximum(jnp.maximum(jnp.max(s_c, axis=-1, keepdims=True),
                                     jnp.max(s_n, axis=-1, keepdims=True)), sink)
        e_c = jnp.exp(s_c - mx)
        e_n = jnp.exp(s_n - mx)
        den = (jnp.sum(e_c, axis=-1, keepdims=True) + jnp.sum(e_n, axis=-1, keepdims=True)
               + jnp.exp(sink - mx))
        o = (jnp.einsum("nqk,nkd->nqd", (e_c / den).astype(BF16), vc, preferred_element_type=F32)
             + jnp.einsum("nqk,nkd->nqd", (e_n / den).astype(BF16), vnw,
                          preferred_element_type=F32))
        for r in range(ATT_REP):
            outs[g * ATT_REP + r] = o[:, r * t_len:(r + 1) * t_len, :].reshape(nb * t_len, HEAD_DIM)
    ocat_ref[:, 1024:1536] = jnp.concatenate(outs, axis=-1).astype(BF16)


def _sample_mixer(ocat, proj, s0, ck, cv, row0, tabs, lw):
    ns, wb = ck.shape[0], ck.shape[1]
    nb = SEQ_PER_STEP
    t_len = CHUNK // nb
    blk0 = row0 // CHUNK
    const2 = lambda i: (0, 0)
    const3 = lambda i: (0, 0, 0)
    row_blk = lambda i: (blk0 + i, 0)
    seq_blk3 = lambda i: (i, 0, 0)
    return pl.pallas_call(
        _sample_mixer_kernel,
        grid=(ns // nb,),
        in_specs=[pl.BlockSpec(memory_space=pl.ANY),
                  pl.BlockSpec((CHUNK, MIX_WIDTH), row_blk),
                  pl.BlockSpec((nb, RET_HEADS, HEAD_DIM, HEAD_DIM), lambda i: (i, 0, 0, 0)),
                  pl.BlockSpec((nb, wb, 128), seq_blk3),
                  pl.BlockSpec((nb, wb, 128), seq_blk3),
                  pl.BlockSpec((CHUNK, 512), const2),
                  pl.BlockSpec((CHUNK, 512), const2),
                  pl.BlockSpec((GM_GROUPS, CHUNK, CHUNK), const3),
                  pl.BlockSpec((GM_GROUPS, CHUNK, CHUNK), const3),
                  pl.BlockSpec((1, 512), const2),
                  pl.BlockSpec((1, 512), const2),
                  pl.BlockSpec((RET_HEADS, CHUNK, CHUNK), const3),
                  pl.BlockSpec((CHUNK, 512), const2),
                  pl.BlockSpec((CHUNK, 512), const2),
                  pl.BlockSpec((RET_HEADS, HEAD_DIM, HEAD_DIM), const3),
                  pl.BlockSpec((1, 512), const2),
                  pl.BlockSpec(memory_space=pltpu.SMEM)],
        out_specs=[pl.BlockSpec((CHUNK, 3 * 512), row_blk),
                   pl.BlockSpec((nb, RET_HEADS, HEAD_DIM, HEAD_DIM), lambda i: (i, 0, 0, 0)),
                   pl.BlockSpec((nb, wb, 128), seq_blk3),
                   pl.BlockSpec((nb, wb, 128), seq_blk3),
                   pl.BlockSpec((CHUNK, GM_WIDTH), lambda i: (i, 0))],
        out_shape=[jax.ShapeDtypeStruct(ocat.shape, BF16),
                   jax.ShapeDtypeStruct(s0.shape, F32),
                   jax.ShapeDtypeStruct(ck.shape, F32),
                   jax.ShapeDtypeStruct(cv.shape, F32),
                   jax.ShapeDtypeStruct((ns * t_len, GM_WIDTH), F32)],
        input_output_aliases={0: 0},
        compiler_params=pltpu.CompilerParams(dimension_semantics=("parallel",),
                                             vmem_limit_bytes=VMEM_LIMIT),
        name="sample_mixer",
    )(ocat, proj, s0, ck, cv, tabs["cos_s"], tabs["sin_s"], lw["wm_s"], lw["gb_s"],
      lw["ln_w"], lw["ln_b"], tabs["dmat_s"], tabs["qdec_s"], tabs["kdec_s"], tabs["sdec_s"],
      lw["gn_w"], lw["sinks"])


def _gate_out_kernel(x_ref, oc_ref, nw_ref, wg_ref, bg_ref, wbr_ref, wout_ref, h_ref):
    x = x_ref[...]
    xn = _rms(x, nw_ref[...]).astype(BF16)
    m = None
    for b in range(3):
        cols = slice(b * D_MODEL, (b + 1) * D_MODEL)
        gate = jax.nn.sigmoid(jnp.dot(xn, wg_ref[:, cols], preferred_element_type=F32)
                              + bg_ref[:, cols])
        p = jnp.dot(oc_ref[:, b * 512:(b + 1) * 512], wbr_ref[b * 512:(b + 1) * 512, :],
                    preferred_element_type=F32)
        m = gate * p if m is None else m + gate * p
    h_ref[...] = x + jnp.dot(m.astype(BF16), wout_ref[...], preferred_element_type=F32)


def _gate_out(x, ocat, nw, wg, bg, wbr, wout):
    m, d = x.shape
    tm = _tile(m, 512)
    const = lambda i: (0, 0)
    return pl.pallas_call(
        _gate_out_kernel,
        grid=(m // tm,),
        in_specs=[pl.BlockSpec((tm, d), lambda i: (i, 0)),
                  pl.BlockSpec((tm, 3 * 512), lambda i: (i, 0)),
                  pl.BlockSpec((1, d), const),
                  pl.BlockSpec((d, GATE_WIDTH), const),
                  pl.BlockSpec((1, GATE_WIDTH), const),
                  pl.BlockSpec((3 * 512, d), const),
                  pl.BlockSpec((d, d), const)],
        out_specs=pl.BlockSpec((tm, d), lambda i: (i, 0)),
        out_shape=jax.ShapeDtypeStruct((m, d), F32),
        compiler_params=pltpu.CompilerParams(dimension_semantics=("parallel",),
                                             vmem_limit_bytes=VMEM_LIMIT),
        name="gate_out",
    )(x, ocat, nw, wg, bg, wbr, wout)


def _col_max(w):
    return jnp.max(w, axis=0, keepdims=True)


def _top_sorted(w):
    row = lax.broadcasted_iota(jnp.int32, (PEER_TOPK, w.shape[1]), 0)
    out = jnp.zeros((PEER_TOPK, w.shape[1]), F32)
    for a in range(PEER_TOPK):
        m = _col_max(w)
        out = jnp.where(row == a, m, out)
        if a + 1 < PEER_TOPK:
            w = jnp.where(w == m, -jnp.inf, w)
    return out


def _select_threshold(v1, v2):
    blocks = [v2 + v1[0:1]]
    for a in range(1, 8):
        blocks.append(v2[0:8] + v1[a:a + 1])
    blocks.append(v1[8:16] + v2[0:1])
    cand = jnp.concatenate(blocks, axis=0)
    w = cand
    for _ in range(PEER_TOPK - 1):
        w = jnp.where(w == _col_max(w), -jnp.inf, w)
    tau = _col_max(w)
    top = v1[0:1] + v2[0:1]
    e = jnp.where(cand >= tau, jnp.exp(cand - top), 0.0)
    return tau, 1.0 / jnp.sum(e, axis=0, keepdims=True)


def _peer_kernel(*refs, final_norm):
    if final_norm:
        (h_ref, nw_ref, wqt_ref, k1_ref, k2_ref, pu_ref, pvt_ref, nf_ref, out_ref,
         xnt_scr, s1_scr, s2_scr, a_scr, b_scr, tau_scr, act_scr, z_scr, acc_scr) = refs
    else:
        (h_ref, nw_ref, wqt_ref, k1_ref, k2_ref, pu_ref, pvt_ref, out_ref,
         xnt_scr, s1_scr, s2_scr, a_scr, b_scr, tau_scr, act_scr, z_scr, acc_scr) = refs
        nf_ref = None
    j = pl.program_id(1)
    tm = h_ref.shape[0]
    nlc = tm // LANES
    eb = pu_ref.shape[0]
    nsub = eb // PEER_NKEYS
    dq = 2 * PEER_NKEYS

    @pl.when(j == 0)
    def _prepare():
        xn = _rms(h_ref[...], nw_ref[...])
        xnt = xn.T.astype(BF16)
        xnt_scr[...] = xnt
        qt = jnp.dot(wqt_ref[...], xnt, preferred_element_type=F32).astype(BF16)
        for h in range(PEER_HEADS):
            s1 = jnp.dot(k1_ref[...], qt[h * dq:h * dq + PEER_NKEYS], preferred_element_type=F32)
            s2 = jnp.dot(k2_ref[...], qt[h * dq + PEER_NKEYS:(h + 1) * dq],
                         preferred_element_type=F32)
            s1_scr[h] = s1
            s2_scr[h] = s2
            for lc in range(nlc):
                cs = slice(lc * LANES, (lc + 1) * LANES)
                v1 = _top_sorted(s1[:, cs])
                v2 = _top_sorted(s2[:, cs])
                tau, zinv = _select_threshold(v1, v2)
                tau_scr[h:h + 1, cs] = tau
                a_scr[h, :, cs] = jnp.exp(s1[:, cs] - v1[0:1])
                b_scr[h, :, cs] = jnp.exp(s2[:, cs] - v2[0:1]) * zinv
        acc_scr[...] = jnp.zeros_like(acc_scr)

    act_scr[...] = jnp.dot(pu_ref[...], xnt_scr[...], preferred_element_type=F32)
    base = pl.multiple_of(j * nsub, nsub)

    for lc in range(nlc):
        cs = slice(lc * LANES, (lc + 1) * LANES)
        s1rows = [s1_scr[h, pl.ds(base, nsub), cs] for h in range(PEER_HEADS)]
        arows = [a_scr[h, pl.ds(base, nsub), cs] for h in range(PEER_HEADS)]
        for ii in range(nsub):
            rs = slice(ii * PEER_NKEYS, (ii + 1) * PEER_NKEYS)
            w = jnp.zeros((PEER_NKEYS, LANES), F32)
            for h in range(PEER_HEADS):
                hit = (s2_scr[h, :, cs] + s1rows[h][ii:ii + 1]) >= tau_scr[h:h + 1, cs]
                w = w + jnp.where(hit, b_scr[h, :, cs], 0.0) * arows[h][ii:ii + 1]
            z_scr[rs, cs] = (w * _gelu(act_scr[rs, cs])).astype(BF16)
    acc_scr[...] += jnp.dot(pvt_ref[...], z_scr[...], preferred_element_type=F32)

    @pl.when(j == pl.num_programs(1) - 1)
    def _finish():
        y = h_ref[...] + acc_scr[...].T
        if final_norm:
            y = _rms(y, nf_ref[...])
        out_ref[...] = y


def _peer(h, nw, wqt, k1, k2, pu, pvt, nf):
    m, d = h.shape
    ne = pu.shape[0]
    tm = _tile(m, 512)
    eb = PEER_EXPERT_BLOCK
    const = lambda i, j: (0, 0)
    in_specs = [pl.BlockSpec((tm, d), lambda i, j: (i, 0)),
                pl.BlockSpec((1, d), const),
                pl.BlockSpec(wqt.shape, const),
                pl.BlockSpec(k1.shape, const),
                pl.BlockSpec(k2.shape, const),
                pl.BlockSpec((eb, d), lambda i, j: (j, 0)),
                pl.BlockSpec((d, eb), lambda i, j: (0, j))]
    args = [h, nw, wqt, k1, k2, pu, pvt]
    if nf is not None:
        in_specs.append(pl.BlockSpec((1, d), const))
        args.append(nf)
    sel = pltpu.VMEM((PEER_HEADS, PEER_NKEYS, tm), F32)
    return pl.pallas_call(
        functools.partial(_peer_kernel, final_norm=nf is not None),
        grid=(m // tm, ne // eb),
        in_specs=in_specs,
        out_specs=pl.BlockSpec((tm, d), lambda i, j: (i, 0)),
        out_shape=jax.ShapeDtypeStruct((m, d), F32),
        scratch_shapes=[pltpu.VMEM((d, tm), BF16), sel, sel, sel, sel,
                        pltpu.VMEM((PEER_HEADS, tm), F32),
                        pltpu.VMEM((eb, tm), F32),
                        pltpu.VMEM((eb, tm), BF16),
                        pltpu.VMEM((d, tm), F32)],
        compiler_params=pltpu.CompilerParams(dimension_semantics=("parallel", "arbitrary"),
                                             vmem_limit_bytes=VMEM_LIMIT),
        name="peer",
    )(*args)


def _rope_tables(pos):
    half = HEAD_DIM // 2
    inv = jnp.exp(-math.log(ROPE_THETA) * jnp.arange(half, dtype=F32) / half)
    ang = pos.astype(F32)[:, None] * inv[None, :]
    cos, sin = jnp.cos(ang), jnp.sin(ang)
    return (jnp.tile(jnp.concatenate([cos, cos], axis=-1), (1, RET_HEADS)),
            jnp.tile(jnp.concatenate([-sin, sin], axis=-1), (1, RET_HEADS)))


def _decay_tables(t_idx, same_seq, c_len):
    lg = jnp.log1p(-jnp.exp2(-5.0 - jnp.arange(RET_HEADS, dtype=F32)))[:, None]
    t = t_idx.astype(F32)
    diff = t[:, None] - t[None, :]
    dmat = jnp.where((diff[None] >= 0) & same_seq[None],
                     jnp.exp(jnp.maximum(diff, 0.0)[None] * lg[:, :, None]), 0.0)
    qdec = jnp.repeat(jnp.exp((t + 1.0)[None] * lg).T, HEAD_DIM, axis=1)
    kdec = jnp.repeat(jnp.exp((c_len - 1.0 - t)[None] * lg).T, HEAD_DIM, axis=1)
    sdec = jnp.broadcast_to(jnp.exp(c_len * lg)[:, :, None], (RET_HEADS, HEAD_DIM, HEAD_DIM))
    return dmat, qdec, kdec, sdec


def kernel(x_prompt, x_sample, state_ret, cache_win_k, cache_win_v, norm1_w, norm2_w, normf_w, w_in, b_gate, gm_ln_w, gm_ln_b, gm_ws, gm_b, ret_gn_w, attn_sinks, w_br_a, w_br_b, w_br_c, w_out, peer_wq, peer_k1, peer_k2, peer_u, peer_v):
    batch, seq, d = x_prompt.shape
    ns, t_len, _ = x_sample.shape
    depth = w_in.shape[0]
    wb = cache_win_k.shape[2]
    assert d == D_MODEL and seq % CHUNK == 0 and t_len * SEQ_PER_STEP == CHUNK
    assert ns % SEQ_PER_STEP == 0 and wb == CHUNK
    mp, ms = batch * seq, ns * t_len
    m = mp + ms
    x = jnp.concatenate([x_prompt.reshape(mp, d), x_sample.reshape(ms, d)], axis=0)

    row = jnp.arange(CHUNK)
    tabs = {}
    tabs["cos_p"], tabs["sin_p"] = _rope_tables(jnp.arange(seq))
    tabs["cos_s"], tabs["sin_s"] = _rope_tables(PAST_LEN + row % t_len)
    (tabs["dmat_p"], tabs["qdec_p"], tabs["kdec_p"], tabs["sdec_p"]) = _decay_tables(
        row, jnp.ones((CHUNK, CHUNK), bool), float(CHUNK))
    (tabs["dmat_s"], tabs["qdec_s"], tabs["kdec_s"], tabs["sdec_s"]) = _decay_tables(
        row % t_len, (row // t_len)[:, None] == (row // t_len)[None, :], float(t_len))
    causal = jnp.tril(jnp.ones((CHUNK, CHUNK), bool))
    eye = jnp.eye(SEQ_PER_STEP, dtype=F32)

    s_p, s_s, k_p, v_p, k_s, v_s, g_s = [], [], [], [], [], [], []
    for l in range(depth):
        ws = gm_ws[l]
        causal_s = jnp.tril(jnp.ones((t_len, t_len), bool))
        ws_s = jnp.where(causal_s[None], ws[:, :t_len, :t_len], 0.0)
        lw = {
            "wm_p": jnp.where(causal[None], ws, 0.0).astype(BF16),
            "gb_p": jnp.broadcast_to(gm_b[l][:, :, None], (GM_GROUPS, CHUNK, CHUNK)),
            "wm_s": jnp.stack([jnp.kron(eye, ws_s[g]) for g in range(GM_GROUPS)]).astype(BF16),
            "gb_s": jnp.broadcast_to(jnp.tile(gm_b[l][:, :t_len], (1, SEQ_PER_STEP))[:, :, None],
                                     (GM_GROUPS, CHUNK, CHUNK)),
            "ln_w": gm_ln_w[l][None], "ln_b": gm_ln_b[l][None],
            "gn_w": ret_gn_w[l][None], "sinks": attn_sinks[l],
        }
        w_in_l = w_in[l].astype(BF16)
        proj = _in_proj(x, norm1_w[l][None], w_in_l[:, :MIX_WIDTH])
        ocat, sp, kp, vp = _prompt_mixer(proj, m, batch, seq, tabs, lw)
        ocat, ss, ks, vs, gs = _sample_mixer(
            ocat, proj, state_ret[l], cache_win_k[l].reshape(ns, wb, 128),
            cache_win_v[l].reshape(ns, wb, 128), mp, tabs, lw)
        wbr = jnp.concatenate([w_br_a[l], w_br_b[l], w_br_c[l]], axis=0).astype(BF16)
        hmid = _gate_out(x, ocat, norm1_w[l][None], w_in_l[:, MIX_WIDTH:], b_gate[l][None],
                         wbr, w_out[l].astype(BF16))
        x = _peer(hmid, norm2_w[l][None], peer_wq[l].T.astype(BF16),
                  peer_k1[l].astype(BF16), peer_k2[l].astype(BF16),
                  peer_u[l].astype(BF16), peer_v[l].T.astype(BF16),
                  normf_w[None] if l == depth - 1 else None)
        s_p.append(sp)
        s_s.append(ss)
        k_p.append(kp.reshape(batch, CHUNK, ATT_KV_HEADS, HEAD_DIM))
        v_p.append(vp.reshape(batch, CHUNK, ATT_KV_HEADS, HEAD_DIM))
        k_s.append(ks.reshape(ns, wb, ATT_KV_HEADS, HEAD_DIM))
        v_s.append(vs.reshape(ns, wb, ATT_KV_HEADS, HEAD_DIM))
        g_s.append(gs.reshape(ns, t_len, GM_WIDTH))
    return (x[:mp].reshape(batch, seq, d), x[mp:].reshape(ns, t_len, d),
            jnp.stack(s_p), jnp.stack(s_s), jnp.stack(k_p), jnp.stack(v_p),
            jnp.stack(k_s), jnp.stack(v_s), jnp.stack(g_s))
```

```python
import functools
import math

import numpy as np
import jax
import jax.numpy as jnp
from jax import lax
from jax.experimental import pallas as pl
from jax.experimental.pallas import tpu as pltpu

F32 = jnp.float32
BF16 = jnp.bfloat16

D_MODEL = 1024
HEAD_DIM = 64
CHUNK = 128
GM_WIDTH = 512
GM_GROUPS = 4
RET_HEADS = 8
ATT_Q_HEADS = 8
ATT_KV_HEADS = 2
ATT_REP = 4
ROPE_THETA = 10000.0
PAST_LEN = 16384
PEER_HEADS = 8
PEER_NKEYS = 128
PEER_TOPK = 16
EPS = 1e-6
NEG = -1e30
MIX_WIDTH = 3840
GATE_WIDTH = 3 * D_MODEL
SEQ_PER_STEP = 16
LANES = 128
PEER_EXPERT_BLOCK = 1024
PEER_QUARTER_KEYS = 2
VMEM_LIMIT = 56 * 1024 * 1024


def _tile(m, target):
    t = min(m, target)
    while m % t or t % LANES:
        t -= LANES
    return t


def _rms(x, w):
    return x * lax.rsqrt(jnp.mean(x * x, axis=-1, keepdims=True) + EPS) * w


def _gelu(x):
    return 0.5 * x * (1.0 + lax.erf(x * np.float32(math.sqrt(0.5))))


def _rope(x, cos, sin_signed):
    w = x.shape[-1]
    lane = lax.broadcasted_iota(jnp.int32, x.shape, 1)
    first = (lane & (HEAD_DIM - 1)) < HEAD_DIM // 2
    rot = jnp.where(first, pltpu.roll(x, w - HEAD_DIM // 2, 1), pltpu.roll(x, HEAD_DIM // 2, 1))
    return x * cos + rot * sin_signed


def _head(x, h):
    return x[:, h * HEAD_DIM:(h + 1) * HEAD_DIM]


def _gmlp(gu, gv, lnw, lnb, wm_ref, gb_ref):
    u = _gelu(gu)
    vf = _gelu(gv)
    mu = jnp.mean(vf, axis=-1, keepdims=True)
    var = jnp.mean(jnp.square(vf - mu), axis=-1, keepdims=True)
    vn = (vf - mu) * lax.rsqrt(var + EPS) * lnw + lnb
    vnb = vn.astype(BF16)
    parts = []
    for g in range(GM_GROUPS):
        s = jnp.dot(wm_ref[g], vnb[:, g * LANES:(g + 1) * LANES], preferred_element_type=F32)
        parts.append(s + gb_ref[g])
    return u * jnp.concatenate(parts, axis=-1), vn


def _group_norm(o):
    mu = jnp.mean(o, axis=-1, keepdims=True)
    var = jnp.mean(jnp.square(o - mu), axis=-1, keepdims=True)
    return (o - mu) * lax.rsqrt(var + EPS)


def _dot_nt(a, b):
    return lax.dot_general(a, b, (((1,), (1,)), ((), ())), preferred_element_type=F32)


def _in_proj_kernel(x_ref, nw_ref, w_ref, o_ref):
    xn = _rms(x_ref[...], nw_ref[...]).astype(BF16)
    o_ref[...] = jnp.dot(xn, w_ref[...], preferred_element_type=F32)


def _in_proj(x, nw, w):
    m, d = x.shape
    n = w.shape[1]
    tm = _tile(m, 512)
    return pl.pallas_call(
        _in_proj_kernel,
        grid=(m // tm,),
        in_specs=[pl.BlockSpec((tm, d), lambda i: (i, 0)),
                  pl.BlockSpec((1, d), lambda i: (0, 0)),
                  pl.BlockSpec((d, n), lambda i: (0, 0))],
        out_specs=pl.BlockSpec((tm, n), lambda i: (i, 0)),
        out_shape=jax.ShapeDtypeStruct((m, n), F32),
        compiler_params=pltpu.CompilerParams(dimension_semantics=("parallel",),
                                             vmem_limit_bytes=VMEM_LIMIT),
        name="in_proj",
    )(x, nw, w)


def _prompt_mixer_kernel(proj_ref, cos_ref, sin_ref, wm_ref, gb_ref, lnw_ref, lnb_ref,
                         dmat_ref, qdec_ref, kdec_ref, sdec_ref, gnw_ref, sinks_ref,
                         ocat_ref, sfin_ref, kout_ref, vout_ref,
                         s_scr, kprev_scr, vprev_scr):
    c = pl.program_id(1)

    @pl.when(c == 0)
    def _reset():
        s_scr[...] = jnp.zeros_like(s_scr)
        kprev_scr[...] = jnp.zeros_like(kprev_scr)
        vprev_scr[...] = jnp.zeros_like(vprev_scr)

    cos = cos_ref[...]
    sin = sin_ref[...]

    o_a, _ = _gmlp(proj_ref[:, 0:512], proj_ref[:, 512:1024], lnw_ref[...], lnb_ref[...],
                   wm_ref, gb_ref)
    ocat_ref[:, 0:512] = o_a.astype(BF16)

    q = _rope(proj_ref[:, 1024:1536], cos, sin)
    ks = _rope(proj_ref[:, 1536:2048], cos, sin) * (HEAD_DIM ** -0.5)
    v = proj_ref[:, 2048:2560]
    qd = q * qdec_ref[...]
    kd = ks * kdec_ref[...]
    ys = []
    for h in range(RET_HEADS):
        qh = _head(q, h).astype(BF16)
        kh = _head(ks, h).astype(BF16)
        vh = _head(v, h).astype(BF16)
        sc = _dot_nt(qh, kh) * dmat_ref[h]
        o = jnp.dot(sc.astype(BF16), vh, preferred_element_type=F32)
        s_old = s_scr[h]
        o = o + jnp.dot(_head(qd, h).astype(BF16), s_old.astype(BF16), preferred_element_type=F32)
        kdh = _head(kd, h).astype(BF16)
        s_new = s_old * sdec_ref[h] + lax.dot_general(
            kdh, vh, (((0,), (0,)), ((), ())), preferred_element_type=F32)
        s_scr[h] = s_new
        sfin_ref[h] = s_new
        ys.append(_group_norm(o))
    y = jnp.concatenate(ys, axis=-1) * gnw_ref[...]
    rg = proj_ref[:, 2560:3072]
    ocat_ref[:, 512:1024] = (rg * jax.nn.sigmoid(rg) * y).astype(BF16)

    qa = _rope(proj_ref[:, 3072:3584], cos, sin)
    ka = _rope(proj_ref[:, 3584:3712], cos[:, 0:128], sin[:, 0:128])
    va = proj_ref[:, 3712:3840]
    kout_ref[...] = ka
    vout_ref[...] = va
    rows = ATT_REP * CHUNK
    tt = lax.broadcasted_iota(jnp.int32, (rows, 2 * CHUNK), 0) & (CHUNK - 1)
    kk = lax.broadcasted_iota(jnp.int32, (rows, 2 * CHUNK), 1)
    valid = (kk > tt) & (kk <= tt + CHUNK) & ((kk >= CHUNK) | (c > 0))
    rr = lax.broadcasted_iota(jnp.int32, (rows, 1), 0) // CHUNK
    outs = [None] * ATT_Q_HEADS
    for g in range(ATT_KV_HEADS):
        kcat = jnp.concatenate([_head(kprev_scr[...], g), _head(ka, g)], axis=0).astype(BF16)
        vcat = jnp.concatenate([_head(vprev_scr[...], g), _head(va, g)], axis=0).astype(BF16)
        qg = jnp.concatenate([_head(qa, g * ATT_REP + r) for r in range(ATT_REP)],
                             axis=0).astype(BF16)
        s = _dot_nt(qg, kcat) * (HEAD_DIM ** -0.5)
        s = jnp.where(valid, s, NEG)
        sink = jnp.zeros((rows, 1), F32)
        for r in range(ATT_REP):
            sink = jnp.where(rr == r, sinks_ref[g * ATT_REP + r], sink)
        mx = jnp.maximum(jnp.max(s, axis=-1, keepdims=True), sink)
        e = jnp.exp(s - mx)
        den = jnp.sum(e, axis=-1, keepdims=True) + jnp.exp(sink - mx)
        p = (e / den).astype(BF16)
        o = jnp.dot(p, vcat, preferred_element_type=F32)
        for r in range(ATT_REP):
            outs[g * ATT_REP + r] = o[r * CHUNK:(r + 1) * CHUNK]
    ocat_ref[:, 1024:1536] = jnp.concatenate(outs, axis=-1).astype(BF16)
    kprev_scr[...] = ka
    vprev_scr[...] = va


def _prompt_mixer(proj, m_total, batch, seq, tabs, lw):
    nchunk = seq // CHUNK
    const2 = lambda b, c: (0, 0)
    const3 = lambda b, c: (0, 0, 0)
    row_blk = lambda b, c: (b * nchunk + c, 0)
    return pl.pallas_call(
        _prompt_mixer_kernel,
        grid=(batch, nchunk),
        in_specs=[pl.BlockSpec((CHUNK, MIX_WIDTH), row_blk),
                  pl.BlockSpec((CHUNK, 512), lambda b, c: (c, 0)),
                  pl.BlockSpec((CHUNK, 512), lambda b, c: (c, 0)),
                  pl.BlockSpec((GM_GROUPS, CHUNK, CHUNK), const3),
                  pl.BlockSpec((GM_GROUPS, CHUNK, CHUNK), const3),
                  pl.BlockSpec((1, 512), const2),
                  pl.BlockSpec((1, 512), const2),
                  pl.BlockSpec((RET_HEADS, CHUNK, CHUNK), const3),
                  pl.BlockSpec((CHUNK, 512), const2),
                  pl.BlockSpec((CHUNK, 512), const2),
                  pl.BlockSpec((RET_HEADS, HEAD_DIM, HEAD_DIM), const3),
                  pl.BlockSpec((1, 512), const2),
                  pl.BlockSpec(memory_space=pltpu.SMEM)],
        out_specs=[pl.BlockSpec((CHUNK, 3 * 512), row_blk),
                   pl.BlockSpec((None, RET_HEADS, HEAD_DIM, HEAD_DIM), lambda b, c: (b, 0, 0, 0)),
                   pl.BlockSpec((None, CHUNK, 128), lambda b, c: (b, 0, 0)),
                   pl.BlockSpec((None, CHUNK, 128), lambda b, c: (b, 0, 0))],
        out_shape=[jax.ShapeDtypeStruct((m_total, 3 * 512), BF16),
                   jax.ShapeDtypeStruct((batch, RET_HEADS, HEAD_DIM, HEAD_DIM), F32),
                   jax.ShapeDtypeStruct((batch, CHUNK, 128), F32),
                   jax.ShapeDtypeStruct((batch, CHUNK, 128), F32)],
        scratch_shapes=[pltpu.VMEM((RET_HEADS, HEAD_DIM, HEAD_DIM), F32),
                        pltpu.VMEM((CHUNK, 128), F32),
                        pltpu.VMEM((CHUNK, 128), F32)],
        compiler_params=pltpu.CompilerParams(dimension_semantics=("parallel", "arbitrary"),
                                             vmem_limit_bytes=VMEM_LIMIT),
        name="prompt_mixer",
    )(proj, tabs["cos_p"], tabs["sin_p"], lw["wm_p"], lw["gb_p"], lw["ln_w"], lw["ln_b"],
      tabs["dmat_p"], tabs["qdec_p"], tabs["kdec_p"], tabs["sdec_p"], lw["gn_w"], lw["sinks"])


def _sample_mixer_kernel(ocat_in_ref, proj_ref, s0_ref, ck_ref, cv_ref, cos_ref, sin_ref,
                         wm_ref, gb_ref, lnw_ref, lnb_ref, dmat_ref, qdec_ref, kdec_ref,
                         sdec_ref, gnw_ref, sinks_ref,
                         ocat_ref, snew_ref, kout_ref, vout_ref, vn_ref):
    del ocat_in_ref
    nb = s0_ref.shape[0]
    t_len = proj_ref.shape[0] // nb
    cos = cos_ref[...]
    sin = sin_ref[...]

    o_a, vn = _gmlp(proj_ref[:, 0:512], proj_ref[:, 512:1024], lnw_ref[...], lnb_ref[...],
                    wm_ref, gb_ref)
    ocat_ref[:, 0:512] = o_a.astype(BF16)
    vn_ref[...] = vn

    def seq3(x):
        return x.reshape(nb, t_len, x.shape[-1])

    q = _rope(proj_ref[:, 1024:1536], cos, sin)
    ks = _rope(proj_ref[:, 1536:2048], cos, sin) * (HEAD_DIM ** -0.5)
    v = proj_ref[:, 2048:2560]
    qd = q * qdec_ref[...]
    kd = ks * kdec_ref[...]
    ys = []
    for h in range(RET_HEADS):
        qh = _head(q, h).astype(BF16)
        kh = _head(ks, h).astype(BF16)
        vh = _head(v, h).astype(BF16)
        sc = _dot_nt(qh, kh) * dmat_ref[h]
        o = jnp.dot(sc.astype(BF16), vh, preferred_element_type=F32)
        s_old = s0_ref[:, h]
        oc = jnp.einsum("ntd,nde->nte", seq3(_head(qd, h)).astype(BF16), s_old.astype(BF16),
                        preferred_element_type=F32)
        o = o + oc.reshape(nb * t_len, HEAD_DIM)
        kd3t = jnp.swapaxes(seq3(_head(kd, h)), 1, 2).astype(BF16)
        upd = jnp.einsum("ndt,nte->nde", kd3t, seq3(_head(v, h)).astype(BF16),
                         preferred_element_type=F32)
        snew_ref[:, h] = s_old * sdec_ref[h] + upd
        ys.append(_group_norm(o))
    y = jnp.concatenate(ys, axis=-1) * gnw_ref[...]
    rg = proj_ref[:, 2560:3072]
    ocat_ref[:, 512:1024] = (rg * jax.nn.sigmoid(rg) * y).astype(BF16)

    qa = _rope(proj_ref[:, 3072:3584], cos, sin)
    ka = _rope(proj_ref[:, 3584:3712], cos[:, 0:128], sin[:, 0:128])
    va = proj_ref[:, 3712:3840]
    wb = ck_ref.shape[1]
    kout_ref[:, 0:wb - t_len, :] = ck_ref[:, t_len:wb, :]
    kout_ref[:, wb - t_len:wb, :] = seq3(ka)
    vout_ref[:, 0:wb - t_len, :] = cv_ref[:, t_len:wb, :]
    vout_ref[:, wb - t_len:wb, :] = seq3(va)
    nq = ATT_REP * t_len
    tq_c = lax.broadcasted_iota(jnp.int32, (nb, nq, wb), 1) & (t_len - 1)
    kk_c = lax.broadcasted_iota(jnp.int32, (nb, nq, wb), 2)
    valid_c = kk_c > tq_c
    tq_n = lax.broadcasted_iota(jnp.int32, (nb, nq, t_len), 1) & (t_len - 1)
    kk_n = lax.broadcasted_iota(jnp.int32, (nb, nq, t_len), 2)
    valid_n = kk_n <= tq_n
    rr = lax.broadcasted_iota(jnp.int32, (1, nq, 1), 1) // t_len
    outs = [None] * ATT_Q_HEADS
    for g in range(ATT_KV_HEADS):
        gs = slice(g * HEAD_DIM, (g + 1) * HEAD_DIM)
        kc = ck_ref[:, :, gs].astype(BF16)
        vc = cv_ref[:, :, gs].astype(BF16)
        kn = seq3(_head(ka, g)).astype(BF16)
        vnw = seq3(_head(va, g)).astype(BF16)
        qg = jnp.concatenate([seq3(_head(qa, g * ATT_REP + r)) for r in range(ATT_REP)],
                             axis=1).astype(BF16)
        s_c = jnp.einsum("nqd,nkd->nqk", qg, kc, preferred_element_type=F32) * (HEAD_DIM ** -0.5)
        s_n = jnp.einsum("nqd,nkd->nqk", qg, kn, preferred_element_type=F32) * (HEAD_DIM ** -0.5)
        s_c = jnp.where(valid_c, s_c, NEG)
        s_n = jnp.where(valid_n, s_n, NEG)
        sink = jnp.zeros((1, nq, 1), F32)
        for r in range(ATT_REP):
            sink = jnp.where(rr == r, sinks_ref[g * ATT_REP + r], sink)
        mx = jnp.maximum(jnp.maximum(jnp.max(s_c, axis=-1, keepdims=True),
                                     jnp.max(s_n, axis=-1, keepdims=True)), sink)
        e_c = jnp.exp(s_c - mx)
        e_n = jnp.exp(s_n - mx)
        den = (jnp.sum(e_c, axis=-1, keepdims=True) + jnp.sum(e_n, axis=-1, keepdims=True)
               + jnp.exp(sink - mx))
        o = (jnp.einsum("nqk,nkd->nqd", (e_c / den).astype(BF16), vc, preferred_element_type=F32)
             + jnp.einsum("nqk,nkd->nqd", (e_n / den).astype(BF16), vnw,
                          preferred_element_type=F32))
        for r in range(ATT_REP):
            outs[g * ATT_REP + r] = o[:, r * t_len:(r + 1) * t_len, :].reshape(nb * t_len, HEAD_DIM)
    ocat_ref[:, 1024:1536] = jnp.concatenate(outs, axis=-1).astype(BF16)


def _sample_mixer(ocat, proj, s0, ck, cv, row0, tabs, lw):
    ns, wb = ck.shape[0], ck.shape[1]
    nb = SEQ_PER_STEP
    t_len = CHUNK // nb
    blk0 = row0 // CHUNK
    const2 = lambda i: (0, 0)
    const3 = lambda i: (0, 0, 0)
    row_blk = lambda i: (blk0 + i, 0)
    seq_blk3 = lambda i: (i, 0, 0)
    return pl.pallas_call(
        _sample_mixer_kernel,
        grid=(ns // nb,),
        in_specs=[pl.BlockSpec(memory_space=pl.ANY),
                  pl.BlockSpec((CHUNK, MIX_WIDTH), row_blk),
                  pl.BlockSpec((nb, RET_HEADS, HEAD_DIM, HEAD_DIM), lambda i: (i, 0, 0, 0)),
                  pl.BlockSpec((nb, wb, 128), seq_blk3),
                  pl.BlockSpec((nb, wb, 128), seq_blk3),
                  pl.BlockSpec((CHUNK, 512), const2),
                  pl.BlockSpec((CHUNK, 512), const2),
                  pl.BlockSpec((GM_GROUPS, CHUNK, CHUNK), const3),
                  pl.BlockSpec((GM_GROUPS, CHUNK, CHUNK), const3),
                  pl.BlockSpec((1, 512), const2),
                  pl.BlockSpec((1, 512), const2),
                  pl.BlockSpec((RET_HEADS, CHUNK, CHUNK), const3),
                  pl.BlockSpec((CHUNK, 512), const2),
                  pl.BlockSpec((CHUNK, 512), const2),
                  pl.BlockSpec((RET_HEADS, HEAD_DIM, HEAD_DIM), const3),
                  pl.BlockSpec((1, 512), const2),
                  pl.BlockSpec(memory_space=pltpu.SMEM)],
        out_specs=[pl.BlockSpec((CHUNK, 3 * 512), row_blk),
                   pl.BlockSpec((nb, RET_HEADS, HEAD_DIM, HEAD_DIM), lambda i: (i, 0, 0, 0)),
                   pl.BlockSpec((nb, wb, 128), seq_blk3),
                   pl.BlockSpec((nb, wb, 128), seq_blk3),
                   pl.BlockSpec((CHUNK, GM_WIDTH), lambda i: (i, 0))],
        out_shape=[jax.ShapeDtypeStruct(ocat.shape, BF16),
                   jax.ShapeDtypeStruct(s0.shape, F32),
                   jax.ShapeDtypeStruct(ck.shape, F32),
                   jax.ShapeDtypeStruct(cv.shape, F32),
                   jax.ShapeDtypeStruct((ns * t_len, GM_WIDTH), F32)],
        input_output_aliases={0: 0},
        compiler_params=pltpu.CompilerParams(dimension_semantics=("parallel",),
                                             vmem_limit_bytes=VMEM_LIMIT),
        name="sample_mixer",
    )(ocat, proj, s0, ck, cv, tabs["cos_s"], tabs["sin_s"], lw["wm_s"], lw["gb_s"],
      lw["ln_w"], lw["ln_b"], tabs["dmat_s"], tabs["qdec_s"], tabs["kdec_s"], tabs["sdec_s"],
      lw["gn_w"], lw["sinks"])


def _gate_out_kernel(x_ref, oc_ref, nw_ref, wg_ref, bg_ref, wbr_ref, wout_ref, h_ref):
    x = x_ref[...]
    xn = _rms(x, nw_ref[...]).astype(BF16)
    m = None
    for b in range(3):
        cols = slice(b * D_MODEL, (b + 1) * D_MODEL)
        gate = jax.nn.sigmoid(jnp.dot(xn, wg_ref[:, cols], preferred_element_type=F32)
                              + bg_ref[:, cols])
        p = jnp.dot(oc_ref[:, b * 512:(b + 1) * 512], wbr_ref[b * 512:(b + 1) * 512, :],
                    preferred_element_type=F32)
        m = gate * p if m is None else m + gate * p
    h_ref[...] = x + jnp.dot(m.astype(BF16), wout_ref[...], preferred_element_type=F32)


def _gate_out(x, ocat, nw, wg, bg, wbr, wout):
    m, d = x.shape
    tm = _tile(m, 512)
    const = lambda i: (0, 0)
    return pl.pallas_call(
        _gate_out_kernel,
        grid=(m // tm,),
        in_specs=[pl.BlockSpec((tm, d), lambda i: (i, 0)),
                  pl.BlockSpec((tm, 3 * 512), lambda i: (i, 0)),
                  pl.BlockSpec((1, d), const),
                  pl.BlockSpec((d, GATE_WIDTH), const),
                  pl.BlockSpec((1, GATE_WIDTH), const),
                  pl.BlockSpec((3 * 512, d), const),
                  pl.BlockSpec((d, d), const)],
        out_specs=pl.BlockSpec((tm, d), lambda i: (i, 0)),
        out_shape=jax.ShapeDtypeStruct((m, d), F32),
        compiler_params=pltpu.CompilerParams(dimension_semantics=("parallel",),
                                             vmem_limit_bytes=VMEM_LIMIT),
        name="gate_out",
    )(x, ocat, nw, wg, bg, wbr, wout)


def _col_max(w):
    return jnp.max(w, axis=0, keepdims=True)


def _top_sorted(w, with_rank):
    row = lax.broadcasted_iota(jnp.int32, (PEER_TOPK, w.shape[1]), 0)
    out = jnp.zeros((PEER_TOPK, w.shape[1]), F32)
    rank = jnp.full(w.shape, float(PEER_TOPK), F32)
    for a in range(PEER_TOPK):
        m = _col_max(w)
        out = jnp.where(row == a, m, out)
        top = w == m
        if with_rank:
            rank = jnp.where(top, float(a), rank)
        if a + 1 < PEER_TOPK:
            w = jnp.where(top, -jnp.inf, w)
    return out, rank


def _select_counts(v1, v2):
    blocks = [v2 + v1[0:1]]
    for a in range(1, 8):
        blocks.append(v2[0:8] + v1[a:a + 1])
    blocks.append(v1[8:16] + v2[0:1])
    cand = jnp.concatenate(blocks, axis=0)
    w = cand
    for _ in range(PEER_TOPK - 1):
        w = jnp.where(w == _col_max(w), -jnp.inf, w)
    tau = _col_max(w)
    chosen = cand >= tau
    e = jnp.where(chosen, jnp.exp(cand - (v1[0:1] + v2[0:1])), 0.0)
    ones = jnp.where(chosen, 1.0, 0.0)
    row = lax.broadcasted_iota(jnp.int32, (8, v1.shape[1]), 0)
    low = jnp.zeros((8, v1.shape[1]), F32)
    for a in range(8):
        lo = 0 if a == 0 else 8 + 8 * a
        hi = 16 if a == 0 else lo + 8
        low = jnp.where(row == a, jnp.sum(ones[lo:hi], axis=0, keepdims=True), low)
    counts = jnp.concatenate([low, ones[72:80]], axis=0)
    return counts, 1.0 / jnp.sum(e, axis=0, keepdims=True)


def _peer_kernel(*refs, final_norm):
    if final_norm:
        (h_ref, nw_ref, wqt_ref, k1_ref, k2_ref, pu_ref, pvt_ref, nf_ref, out_ref,
         xnt_scr, cnt_scr, a_scr, rank_scr, b_scr, z_scr, acc_scr) = refs
    else:
        (h_ref, nw_ref, wqt_ref, k1_ref, k2_ref, pu_ref, pvt_ref, out_ref,
         xnt_scr, cnt_scr, a_scr, rank_scr, b_scr, z_scr, acc_scr) = refs
        nf_ref = None
    j = pl.program_id(1)
    tm = h_ref.shape[0]
    nlc = tm // LANES
    nsub = pu_ref.shape[0] // PEER_NKEYS
    dq = 2 * PEER_NKEYS
    quarter = PEER_QUARTER_KEYS * PEER_NKEYS

    @pl.when(j == 0)
    def _prepare():
        xn = _rms(h_ref[...], nw_ref[...])
        xnt = xn.T.astype(BF16)
        xnt_scr[...] = xnt
        qt = jnp.dot(wqt_ref[...], xnt, preferred_element_type=F32).astype(BF16)
        for h in range(PEER_HEADS):
            s1 = jnp.dot(k1_ref[...], qt[h * dq:h * dq + PEER_NKEYS], preferred_element_type=F32)
            s2 = jnp.dot(k2_ref[...], qt[h * dq + PEER_NKEYS:(h + 1) * dq],
                         preferred_element_type=F32)
            for lc in range(nlc):
                cs = slice(lc * LANES, (lc + 1) * LANES)
                s1c = s1[:, cs]
                s2c = s2[:, cs]
                v1, _ = _top_sorted(s1c, False)
                v2, rank2 = _top_sorted(s2c, True)
                counts, zinv = _select_counts(v1, v2)
                cnt = jnp.zeros_like(s1c)
                for a in range(PEER_TOPK):
                    cnt = jnp.where(s1c == v1[a:a + 1], counts[a:a + 1], cnt)
                cnt_scr[h, :, cs] = cnt
                a_scr[h, :, cs] = jnp.exp(s1c - v1[0:1])
                rank_scr[h, :, cs] = rank2.astype(BF16)
                b_scr[h, :, cs] = (jnp.exp(s2c - v2[0:1]) * zinv).astype(BF16)
        acc_scr[...] = jnp.zeros_like(acc_scr)

    base = pl.multiple_of(j * nsub, nsub)

    def activations(q):
        return jnp.dot(pu_ref[q * quarter:(q + 1) * quarter, :], xnt_scr[...],
                       preferred_element_type=F32)

    def weigh(act, q):
        for lc in range(nlc):
            cs = slice(lc * LANES, (lc + 1) * LANES)
            crows = [cnt_scr[h, pl.ds(base, nsub), cs] for h in range(PEER_HEADS)]
            arows = [a_scr[h, pl.ds(base, nsub), cs] for h in range(PEER_HEADS)]
            for ii in range(PEER_QUARTER_KEYS):
                r = q * PEER_QUARTER_KEYS + ii
                w = jnp.zeros((PEER_NKEYS, LANES), BF16)
                for h in range(PEER_HEADS):
                    hit = rank_scr[h, :, cs] < crows[h][r:r + 1].astype(BF16)
                    w = w + (jnp.where(hit, b_scr[h, :, cs], jnp.zeros((), BF16))
                             * arows[h][r:r + 1].astype(BF16))
                g = _gelu(act[ii * PEER_NKEYS:(ii + 1) * PEER_NKEYS, cs]).astype(BF16)
                z_scr[r * PEER_NKEYS:(r + 1) * PEER_NKEYS, cs] = w * g

    nq = nsub // PEER_QUARTER_KEYS
    act = activations(0)
    total = None
    for q in range(nq):
        act_next = activations(q + 1) if q + 1 < nq else None
        weigh(act, q)
        part = jnp.dot(pvt_ref[:, q * quarter:(q + 1) * quarter],
                       z_scr[q * quarter:(q + 1) * quarter, :], preferred_element_type=F32)
        total = part if total is None else total + part
        act = act_next
    acc_scr[...] += total

    @pl.when(j == pl.num_programs(1) - 1)
    def _finish():
        y = h_ref[...] + acc_scr[...].T
        if final_norm:
            y = _rms(y, nf_ref[...])
        out_ref[...] = y


def _peer(h, nw, wqt, k1, k2, pu, pvt, nf):
    m, d = h.shape
    ne = pu.shape[0]
    tm = _tile(m, 512)
    eb = PEER_EXPERT_BLOCK
    assert eb == 8 * PEER_NKEYS and ne % eb == 0
    const = lambda i, j: (0, 0)
    in_specs = [pl.BlockSpec((tm, d), lambda i, j: (i, 0)),
                pl.BlockSpec((1, d), const),
                pl.BlockSpec(wqt.shape, const),
                pl.BlockSpec(k1.shape, const),
                pl.BlockSpec(k2.shape, const),
                pl.BlockSpec((eb, d), lambda i, j: (j, 0)),
                pl.BlockSpec((d, eb), lambda i, j: (0, j))]
    args = [h, nw, wqt, k1, k2, pu, pvt]
    if nf is not None:
        in_specs.append(pl.BlockSpec((1, d), const))
        args.append(nf)
    sel = pltpu.VMEM((PEER_HEADS, PEER_NKEYS, tm), F32)
    sel16 = pltpu.VMEM((PEER_HEADS, PEER_NKEYS, tm), BF16)
    return pl.pallas_call(
        functools.partial(_peer_kernel, final_norm=nf is not None),
        grid=(m // tm, ne // eb),
        in_specs=in_specs,
        out_specs=pl.BlockSpec((tm, d), lambda i, j: (i, 0)),
        out_shape=jax.ShapeDtypeStruct((m, d), F32),
        scratch_shapes=[pltpu.VMEM((d, tm), BF16), sel, sel, sel16, sel16,
                        pltpu.VMEM((eb, tm), BF16),
                        pltpu.VMEM((d, tm), F32)],
        compiler_params=pltpu.CompilerParams(dimension_semantics=("parallel", "arbitrary"),
                                             vmem_limit_bytes=VMEM_LIMIT),
        name="peer",
    )(*args)


def _rope_tables(pos):
    half = HEAD_DIM // 2
    inv = jnp.exp(-math.log(ROPE_THETA) * jnp.arange(half, dtype=F32) / half)
    ang = pos.astype(F32)[:, None] * inv[None, :]
    cos, sin = jnp.cos(ang), jnp.sin(ang)
    return (jnp.tile(jnp.concatenate([cos, cos], axis=-1), (1, RET_HEADS)),
            jnp.tile(jnp.concatenate([-sin, sin], axis=-1), (1, RET_HEADS)))


def _decay_tables(t_idx, same_seq, c_len):
    lg = jnp.log1p(-jnp.exp2(-5.0 - jnp.arange(RET_HEADS, dtype=F32)))[:, None]
    t = t_idx.astype(F32)
    diff = t[:, None] - t[None, :]
    dmat = jnp.where((diff[None] >= 0) & same_seq[None],
                     jnp.exp(jnp.maximum(diff, 0.0)[None] * lg[:, :, None]), 0.0)
    qdec = jnp.repeat(jnp.exp((t + 1.0)[None] * lg).T, HEAD_DIM, axis=1)
    kdec = jnp.repeat(jnp.exp((c_len - 1.0 - t)[None] * lg).T, HEAD_DIM, axis=1)
    sdec = jnp.broadcast_to(jnp.exp(c_len * lg)[:, :, None], (RET_HEADS, HEAD_DIM, HEAD_DIM))
    return dmat, qdec, kdec, sdec


def kernel(x_prompt, x_sample, state_ret, cache_win_k, cache_win_v, norm1_w, norm2_w, normf_w, w_in, b_gate, gm_ln_w, gm_ln_b, gm_ws, gm_b, ret_gn_w, attn_sinks, w_br_a, w_br_b, w_br_c, w_out, peer_wq, peer_k1, peer_k2, peer_u, peer_v):
    batch, seq, d = x_prompt.shape
    ns, t_len, _ = x_sample.shape
    depth = w_in.shape[0]
    wb = cache_win_k.shape[2]
    assert d == D_MODEL and seq % CHUNK == 0 and t_len * SEQ_PER_STEP == CHUNK
    assert ns % SEQ_PER_STEP == 0 and wb == CHUNK
    mp, ms = batch * seq, ns * t_len
    m = mp + ms
    x = jnp.concatenate([x_prompt.reshape(mp, d), x_sample.reshape(ms, d)], axis=0)

    row = jnp.arange(CHUNK)
    tabs = {}
    tabs["cos_p"], tabs["sin_p"] = _rope_tables(jnp.arange(seq))
    tabs["cos_s"], tabs["sin_s"] = _rope_tables(PAST_LEN + row % t_len)
    (tabs["dmat_p"], tabs["qdec_p"], tabs["kdec_p"], tabs["sdec_p"]) = _decay_tables(
        row, jnp.ones((CHUNK, CHUNK), bool), float(CHUNK))
    (tabs["dmat_s"], tabs["qdec_s"], tabs["kdec_s"], tabs["sdec_s"]) = _decay_tables(
        row % t_len, (row // t_len)[:, None] == (row // t_len)[None, :], float(t_len))
    causal = jnp.tril(jnp.ones((CHUNK, CHUNK), bool))
    eye = jnp.eye(SEQ_PER_STEP, dtype=F32)

    s_p, s_s, k_p, v_p, k_s, v_s, g_s = [], [], [], [], [], [], []
    for l in range(depth):
        ws = gm_ws[l]
        causal_s = jnp.tril(jnp.ones((t_len, t_len), bool))
        ws_s = jnp.where(causal_s[None], ws[:, :t_len, :t_len], 0.0)
        lw = {
            "wm_p": jnp.where(causal[None], ws, 0.0).astype(BF16),
            "gb_p": jnp.broadcast_to(gm_b[l][:, :, None], (GM_GROUPS, CHUNK, CHUNK)),
            "wm_s": jnp.stack([jnp.kron(eye, ws_s[g]) for g in range(GM_GROUPS)]).astype(BF16),
            "gb_s": jnp.broadcast_to(jnp.tile(gm_b[l][:, :t_len], (1, SEQ_PER_STEP))[:, :, None],
                                     (GM_GROUPS, CHUNK, CHUNK)),
            "ln_w": gm_ln_w[l][None], "ln_b": gm_ln_b[l][None],
            "gn_w": ret_gn_w[l][None], "sinks": attn_sinks[l],
        }
        w_in_l = w_in[l].astype(BF16)
        proj = _in_proj(x, norm1_w[l][None], w_in_l[:, :MIX_WIDTH])
        ocat, sp, kp, vp = _prompt_mixer(proj, m, batch, seq, tabs, lw)
        ocat, ss, ks, vs, gs = _sample_mixer(
            ocat, proj, state_ret[l], cache_win_k[l].reshape(ns, wb, 128),
            cache_win_v[l].reshape(ns, wb, 128), mp, tabs, lw)
        wbr = jnp.concatenate([w_br_a[l], w_br_b[l], w_br_c[l]], axis=0).astype(BF16)
        hmid = _gate_out(x, ocat, norm1_w[l][None], w_in_l[:, MIX_WIDTH:], b_gate[l][None],
                         wbr, w_out[l].astype(BF16))
        x = _peer(hmid, norm2_w[l][None], peer_wq[l].T.astype(BF16),
                  peer_k1[l].astype(BF16), peer_k2[l].astype(BF16),
                  peer_u[l].astype(BF16), peer_v[l].T.astype(BF16),
                  normf_w[None] if l == depth - 1 else None)
        s_p.append(sp)
        s_s.append(ss)
        k_p.append(kp.reshape(batch, CHUNK, ATT_KV_HEADS, HEAD_DIM))
        v_p.append(vp.reshape(batch, CHUNK, ATT_KV_HEADS, HEAD_DIM))
        k_s.append(ks.reshape(ns, wb, ATT_KV_HEADS, HEAD_DIM))
        v_s.append(vs.reshape(ns, wb, ATT_KV_HEADS, HEAD_DIM))
        g_s.append(gs.reshape(ns, t_len, GM_WIDTH))
    return (x[:mp].reshape(batch, seq, d), x[mp:].reshape(ns, t_len, d),
            jnp.stack(s_p), jnp.stack(s_s), jnp.stack(k_p), jnp.stack(v_p),
            jnp.stack(k_s), jnp.stack(v_s), jnp.stack(g_s))
```

```python
import functools
import math

import numpy as np
import jax
import jax.numpy as jnp
from jax import lax
from jax.experimental import pallas as pl
from jax.experimental.pallas import tpu as pltpu

F32 = jnp.float32
BF16 = jnp.bfloat16

D_MODEL = 1024
HEAD_DIM = 64
CHUNK = 128
GM_WIDTH = 512
GM_GROUPS = 4
RET_HEADS = 8
ATT_Q_HEADS = 8
ATT_KV_HEADS = 2
ATT_REP = 4
ROPE_THETA = 10000.0
PAST_LEN = 16384
PEER_HEADS = 8
PEER_NKEYS = 128
PEER_TOPK = 16
EPS = 1e-6
NEG = -1e30
MIX_WIDTH = 3840
GATE_WIDTH = 3 * D_MODEL
SEQ_PER_STEP = 16
LANES = 128
PEER_EXPERT_BLOCK = 1024
PEER_QUARTER_KEYS = 2
VMEM_LIMIT = 56 * 1024 * 1024


def _tile(m, target):
    t = min(m, target)
    while m % t or t % LANES:
        t -= LANES
    return t


def _rms(x, w):
    return x * lax.rsqrt(jnp.mean(x * x, axis=-1, keepdims=True) + EPS) * w


def _gelu(x):
    return 0.5 * x * (1.0 + lax.erf(x * np.float32(math.sqrt(0.5))))


def _rope(x, cos, sin_signed):
    w = x.shape[-1]
    lane = lax.broadcasted_iota(jnp.int32, x.shape, 1)
    first = (lane & (HEAD_DIM - 1)) < HEAD_DIM // 2
    rot = jnp.where(first, pltpu.roll(x, w - HEAD_DIM // 2, 1), pltpu.roll(x, HEAD_DIM // 2, 1))
    return x * cos + rot * sin_signed


def _head(x, h):
    return x[:, h * HEAD_DIM:(h + 1) * HEAD_DIM]


def _gmlp(gu, gv, lnw, lnb, wm_ref, gb_ref):
    u = _gelu(gu)
    vf = _gelu(gv)
    mu = jnp.mean(vf, axis=-1, keepdims=True)
    var = jnp.mean(jnp.square(vf - mu), axis=-1, keepdims=True)
    vn = (vf - mu) * lax.rsqrt(var + EPS) * lnw + lnb
    vnb = vn.astype(BF16)
    parts = []
    for g in range(GM_GROUPS):
        s = jnp.dot(wm_ref[g], vnb[:, g * LANES:(g + 1) * LANES], preferred_element_type=F32)
        parts.append(s + gb_ref[g])
    return u * jnp.concatenate(parts, axis=-1), vn


def _group_norm(o):
    mu = jnp.mean(o, axis=-1, keepdims=True)
    var = jnp.mean(jnp.square(o - mu), axis=-1, keepdims=True)
    return (o - mu) * lax.rsqrt(var + EPS)


def _dot_nt(a, b):
    return lax.dot_general(a, b, (((1,), (1,)), ((), ())), preferred_element_type=F32)


def _in_proj_kernel(x_ref, nw_ref, w_ref, o_ref):
    xn = _rms(x_ref[...], nw_ref[...]).astype(BF16)
    o_ref[...] = jnp.dot(xn, w_ref[...], preferred_element_type=F32)


def _in_proj(x, nw, w):
    m, d = x.shape
    n = w.shape[1]
    tm = _tile(m, 512)
    return pl.pallas_call(
        _in_proj_kernel,
        grid=(m // tm,),
        in_specs=[pl.BlockSpec((tm, d), lambda i: (i, 0)),
                  pl.BlockSpec((1, d), lambda i: (0, 0)),
                  pl.BlockSpec((d, n), lambda i: (0, 0))],
        out_specs=pl.BlockSpec((tm, n), lambda i: (i, 0)),
        out_shape=jax.ShapeDtypeStruct((m, n), F32),
        compiler_params=pltpu.CompilerParams(dimension_semantics=("parallel",),
                                             vmem_limit_bytes=VMEM_LIMIT),
        name="in_proj",
    )(x, nw, w)


def _prompt_mixer_kernel(proj_ref, cos_ref, sin_ref, wm_ref, gb_ref, lnw_ref, lnb_ref,
                         dmat_ref, qdec_ref, kdec_ref, sdec_ref, gnw_ref, sinks_ref,
                         ocat_ref, sfin_ref, kout_ref, vout_ref,
                         s_scr, kprev_scr, vprev_scr):
    c = pl.program_id(1)

    @pl.when(c == 0)
    def _reset():
        s_scr[...] = jnp.zeros_like(s_scr)
        kprev_scr[...] = jnp.zeros_like(kprev_scr)
        vprev_scr[...] = jnp.zeros_like(vprev_scr)

    cos = cos_ref[...]
    sin = sin_ref[...]

    o_a, _ = _gmlp(proj_ref[:, 0:512], proj_ref[:, 512:1024], lnw_ref[...], lnb_ref[...],
                   wm_ref, gb_ref)
    ocat_ref[:, 0:512] = o_a.astype(BF16)

    q = _rope(proj_ref[:, 1024:1536], cos, sin)
    ks = _rope(proj_ref[:, 1536:2048], cos, sin) * (HEAD_DIM ** -0.5)
    v = proj_ref[:, 2048:2560]
    qd = q * qdec_ref[...]
    kd = ks * kdec_ref[...]
    ys = []
    for h in range(RET_HEADS):
        qh = _head(q, h).astype(BF16)
        kh = _head(ks, h).astype(BF16)
        vh = _head(v, h).astype(BF16)
        sc = _dot_nt(qh, kh) * dmat_ref[h]
        o = jnp.dot(sc.astype(BF16), vh, preferred_element_type=F32)
        s_old = s_scr[h]
        o = o + jnp.dot(_head(qd, h).astype(BF16), s_old.astype(BF16), preferred_element_type=F32)
        kdh = _head(kd, h).astype(BF16)
        s_new = s_old * sdec_ref[h] + lax.dot_general(
            kdh, vh, (((0,), (0,)), ((), ())), preferred_element_type=F32)
        s_scr[h] = s_new
        sfin_ref[h] = s_new
        ys.append(_group_norm(o))
    y = jnp.concatenate(ys, axis=-1) * gnw_ref[...]
    rg = proj_ref[:, 2560:3072]
    ocat_ref[:, 512:1024] = (rg * jax.nn.sigmoid(rg) * y).astype(BF16)

    qa = _rope(proj_ref[:, 3072:3584], cos, sin)
    ka = _rope(proj_ref[:, 3584:3712], cos[:, 0:128], sin[:, 0:128])
    va = proj_ref[:, 3712:3840]
    kout_ref[...] = ka
    vout_ref[...] = va
    rows = ATT_REP * CHUNK
    tt = lax.broadcasted_iota(jnp.int32, (rows, 2 * CHUNK), 0) & (CHUNK - 1)
    kk = lax.broadcasted_iota(jnp.int32, (rows, 2 * CHUNK), 1)
    valid = (kk > tt) & (kk <= tt + CHUNK) & ((kk >= CHUNK) | (c > 0))
    rr = lax.broadcasted_iota(jnp.int32, (rows, 1), 0) // CHUNK
    outs = [None] * ATT_Q_HEADS
    for g in range(ATT_KV_HEADS):
        kcat = jnp.concatenate([_head(kprev_scr[...], g), _head(ka, g)], axis=0).astype(BF16)
        vcat = jnp.concatenate([_head(vprev_scr[...], g), _head(va, g)], axis=0).astype(BF16)
        qg = jnp.concatenate([_head(qa, g * ATT_REP + r) for r in range(ATT_REP)],
                             axis=0).astype(BF16)
        s = _dot_nt(qg, kcat) * (HEAD_DIM ** -0.5)
        s = jnp.where(valid, s, NEG)
        sink = jnp.zeros((rows, 1), F32)
        for r in range(ATT_REP):
            sink = jnp.where(rr == r, sinks_ref[g * ATT_REP + r], sink)
        mx = jnp.maximum(jnp.max(s, axis=-1, keepdims=True), sink)
        e = jnp.exp(s - mx)
        den = jnp.sum(e, axis=-1, keepdims=True) + jnp.exp(sink - mx)
        p = (e / den).astype(BF16)
        o = jnp.dot(p, vcat, preferred_element_type=F32)
        for r in range(ATT_REP):
            outs[g * ATT_REP + r] = o[r * CHUNK:(r + 1) * CHUNK]
    ocat_ref[:, 1024:1536] = jnp.concatenate(outs, axis=-1).astype(BF16)
    kprev_scr[...] = ka
    vprev_scr[...] = va


def _prompt_mixer(proj, batch, seq, tabs, lw):
    nchunk = seq // CHUNK
    const2 = lambda b, c: (0, 0)
    const3 = lambda b, c: (0, 0, 0)
    row_blk = lambda b, c: (b * nchunk + c, 0)
    return pl.pallas_call(
        _prompt_mixer_kernel,
        grid=(batch, nchunk),
        in_specs=[pl.BlockSpec((CHUNK, MIX_WIDTH), row_blk),
                  pl.BlockSpec((CHUNK, 512), lambda b, c: (c, 0)),
                  pl.BlockSpec((CHUNK, 512), lambda b, c: (c, 0)),
                  pl.BlockSpec((GM_GROUPS, CHUNK, CHUNK), const3),
                  pl.BlockSpec((GM_GROUPS, CHUNK, CHUNK), const3),
                  pl.BlockSpec((1, 512), const2),
                  pl.BlockSpec((1, 512), const2),
                  pl.BlockSpec((RET_HEADS, CHUNK, CHUNK), const3),
                  pl.BlockSpec((CHUNK, 512), const2),
                  pl.BlockSpec((CHUNK, 512), const2),
                  pl.BlockSpec((RET_HEADS, HEAD_DIM, HEAD_DIM), const3),
                  pl.BlockSpec((1, 512), const2),
                  pl.BlockSpec(memory_space=pltpu.SMEM)],
        out_specs=[pl.BlockSpec((CHUNK, 3 * 512), row_blk),
                   pl.BlockSpec((None, RET_HEADS, HEAD_DIM, HEAD_DIM), lambda b, c: (b, 0, 0, 0)),
                   pl.BlockSpec((None, CHUNK, 128), lambda b, c: (b, 0, 0)),
                   pl.BlockSpec((None, CHUNK, 128), lambda b, c: (b, 0, 0))],
        out_shape=[jax.ShapeDtypeStruct((batch * seq, 3 * 512), BF16),
                   jax.ShapeDtypeStruct((batch, RET_HEADS, HEAD_DIM, HEAD_DIM), F32),
                   jax.ShapeDtypeStruct((batch, CHUNK, 128), F32),
                   jax.ShapeDtypeStruct((batch, CHUNK, 128), F32)],
        scratch_shapes=[pltpu.VMEM((RET_HEADS, HEAD_DIM, HEAD_DIM), F32),
                        pltpu.VMEM((CHUNK, 128), F32),
                        pltpu.VMEM((CHUNK, 128), F32)],
        compiler_params=pltpu.CompilerParams(dimension_semantics=("parallel", "arbitrary"),
                                             vmem_limit_bytes=VMEM_LIMIT),
        name="prompt_mixer",
    )(proj, tabs["cos_p"], tabs["sin_p"], lw["wm_p"], lw["gb_p"], lw["ln_w"], lw["ln_b"],
      tabs["dmat_p"], tabs["qdec_p"], tabs["kdec_p"], tabs["sdec_p"], lw["gn_w"], lw["sinks"])


def _sample_mixer_kernel(proj_ref, s0_ref, ck_ref, cv_ref, cos_ref, sin_ref,
                         wm_ref, gb_ref, lnw_ref, lnb_ref, dmat_ref, qdec_ref, kdec_ref,
                         sdec_ref, gnw_ref, sinks_ref,
                         ocat_ref, snew_ref, kout_ref, vout_ref, vn_ref):
    nb = s0_ref.shape[0]
    t_len = proj_ref.shape[0] // nb
    cos = cos_ref[...]
    sin = sin_ref[...]

    o_a, vn = _gmlp(proj_ref[:, 0:512], proj_ref[:, 512:1024], lnw_ref[...], lnb_ref[...],
                    wm_ref, gb_ref)
    ocat_ref[:, 0:512] = o_a.astype(BF16)
    vn_ref[...] = vn

    def seq3(x):
        return x.reshape(nb, t_len, x.shape[-1])

    q = _rope(proj_ref[:, 1024:1536], cos, sin)
    ks = _rope(proj_ref[:, 1536:2048], cos, sin) * (HEAD_DIM ** -0.5)
    v = proj_ref[:, 2048:2560]
    qd = q * qdec_ref[...]
    kd = ks * kdec_ref[...]
    ys = []
    for h in range(RET_HEADS):
        qh = _head(q, h).astype(BF16)
        kh = _head(ks, h).astype(BF16)
        vh = _head(v, h).astype(BF16)
        sc = _dot_nt(qh, kh) * dmat_ref[h]
        o = jnp.dot(sc.astype(BF16), vh, preferred_element_type=F32)
        s_old = s0_ref[:, h]
        oc = jnp.einsum("ntd,nde->nte", seq3(_head(qd, h)).astype(BF16), s_old.astype(BF16),
                        preferred_element_type=F32)
        o = o + oc.reshape(nb * t_len, HEAD_DIM)
        kd3t = jnp.swapaxes(seq3(_head(kd, h)), 1, 2).astype(BF16)
        upd = jnp.einsum("ndt,nte->nde", kd3t, seq3(_head(v, h)).astype(BF16),
                         preferred_element_type=F32)
        snew_ref[:, h] = s_old * sdec_ref[h] + upd
        ys.append(_group_norm(o))
    y = jnp.concatenate(ys, axis=-1) * gnw_ref[...]
    rg = proj_ref[:, 2560:3072]
    ocat_ref[:, 512:1024] = (rg * jax.nn.sigmoid(rg) * y).astype(BF16)

    qa = _rope(proj_ref[:, 3072:3584], cos, sin)
    ka = _rope(proj_ref[:, 3584:3712], cos[:, 0:128], sin[:, 0:128])
    va = proj_ref[:, 3712:3840]
    wb = ck_ref.shape[1]
    kout_ref[:, 0:wb - t_len, :] = ck_ref[:, t_len:wb, :]
    kout_ref[:, wb - t_len:wb, :] = seq3(ka)
    vout_ref[:, 0:wb - t_len, :] = cv_ref[:, t_len:wb, :]
    vout_ref[:, wb - t_len:wb, :] = seq3(va)
    nq = ATT_REP * t_len
    tq_c = lax.broadcasted_iota(jnp.int32, (nb, nq, wb), 1) & (t_len - 1)
    kk_c = lax.broadcasted_iota(jnp.int32, (nb, nq, wb), 2)
    valid_c = kk_c > tq_c
    tq_n = lax.broadcasted_iota(jnp.int32, (nb, nq, t_len), 1) & (t_len - 1)
    kk_n = lax.broadcasted_iota(jnp.int32, (nb, nq, t_len), 2)
    valid_n = kk_n <= tq_n
    rr = lax.broadcasted_iota(jnp.int32, (1, nq, 1), 1) // t_len
    outs = [None] * ATT_Q_HEADS
    for g in range(ATT_KV_HEADS):
        gs = slice(g * HEAD_DIM, (g + 1) * HEAD_DIM)
        kc = ck_ref[:, :, gs].astype(BF16)
        vc = cv_ref[:, :, gs].astype(BF16)
        kn = seq3(_head(ka, g)).astype(BF16)
        vnw = seq3(_head(va, g)).astype(BF16)
        qg = jnp.concatenate([seq3(_head(qa, g * ATT_REP + r)) for r in range(ATT_REP)],
                             axis=1).astype(BF16)
        s_c = jnp.einsum("nqd,nkd->nqk", qg, kc, preferred_element_type=F32) * (HEAD_DIM ** -0.5)
        s_n = jnp.einsum("nqd,nkd->nqk", qg, kn, preferred_element_type=F32) * (HEAD_DIM ** -0.5)
        s_c = jnp.where(valid_c, s_c, NEG)
        s_n = jnp.where(valid_n, s_n, NEG)
        sink = jnp.zeros((1, nq, 1), F32)
        for r in range(ATT_REP):
            sink = jnp.where(rr == r, sinks_ref[g * ATT_REP + r], sink)
        mx = jnp.maximum(jnp.maximum(jnp.max(s_c, axis=-1, keepdims=True),
                                     jnp.max(s_n, axis=-1, keepdims=True)), sink)
        e_c = jnp.exp(s_c - mx)
        e_n = jnp.exp(s_n - mx)
        den = (jnp.sum(e_c, axis=-1, keepdims=True) + jnp.sum(e_n, axis=-1, keepdims=True)
               + jnp.exp(sink - mx))
        o = (jnp.einsum("nqk,nkd->nqd", (e_c / den).astype(BF16), vc, preferred_element_type=F32)
             + jnp.einsum("nqk,nkd->nqd", (e_n / den).astype(BF16), vnw,
                          preferred_element_type=F32))
        for r in range(ATT_REP):
            outs[g * ATT_REP + r] = o[:, r * t_len:(r + 1) * t_len, :].reshape(nb * t_len, HEAD_DIM)
    ocat_ref[:, 1024:1536] = jnp.concatenate(outs, axis=-1).astype(BF16)


def _sample_mixer(proj, s0, ck, cv, tabs, lw):
    ns, wb = ck.shape[0], ck.shape[1]
    nb = SEQ_PER_STEP
    t_len = CHUNK // nb
    const2 = lambda i: (0, 0)
    const3 = lambda i: (0, 0, 0)
    row_blk = lambda i: (i, 0)
    seq_blk3 = lambda i: (i, 0, 0)
    return pl.pallas_call(
        _sample_mixer_kernel,
        grid=(ns // nb,),
        in_specs=[pl.BlockSpec((CHUNK, MIX_WIDTH), row_blk),
                  pl.BlockSpec((nb, RET_HEADS, HEAD_DIM, HEAD_DIM), lambda i: (i, 0, 0, 0)),
                  pl.BlockSpec((nb, wb, 128), seq_blk3),
                  pl.BlockSpec((nb, wb, 128), seq_blk3),
                  pl.BlockSpec((CHUNK, 512), const2),
                  pl.BlockSpec((CHUNK, 512), const2),
                  pl.BlockSpec((GM_GROUPS, CHUNK, CHUNK), const3),
                  pl.BlockSpec((GM_GROUPS, CHUNK, CHUNK), const3),
                  pl.BlockSpec((1, 512), const2),
                  pl.BlockSpec((1, 512), const2),
                  pl.BlockSpec((RET_HEADS, CHUNK, CHUNK), const3),
                  pl.BlockSpec((CHUNK, 512), const2),
                  pl.BlockSpec((CHUNK, 512), const2),
                  pl.BlockSpec((RET_HEADS, HEAD_DIM, HEAD_DIM), const3),
                  pl.BlockSpec((1, 512), const2),
                  pl.BlockSpec(memory_space=pltpu.SMEM)],
        out_specs=[pl.BlockSpec((CHUNK, 3 * 512), row_blk),
                   pl.BlockSpec((nb, RET_HEADS, HEAD_DIM, HEAD_DIM), lambda i: (i, 0, 0, 0)),
                   pl.BlockSpec((nb, wb, 128), seq_blk3),
                   pl.BlockSpec((nb, wb, 128), seq_blk3),
                   pl.BlockSpec((CHUNK, GM_WIDTH), lambda i: (i, 0))],
        out_shape=[jax.ShapeDtypeStruct((ns * t_len, 3 * 512), BF16),
                   jax.ShapeDtypeStruct(s0.shape, F32),
                   jax.ShapeDtypeStruct(ck.shape, F32),
                   jax.ShapeDtypeStruct(cv.shape, F32),
                   jax.ShapeDtypeStruct((ns * t_len, GM_WIDTH), F32)],
        compiler_params=pltpu.CompilerParams(dimension_semantics=("parallel",),
                                             vmem_limit_bytes=VMEM_LIMIT),
        name="sample_mixer",
    )(proj, s0, ck, cv, tabs["cos_s"], tabs["sin_s"], lw["wm_s"], lw["gb_s"],
      lw["ln_w"], lw["ln_b"], tabs["dmat_s"], tabs["qdec_s"], tabs["kdec_s"], tabs["sdec_s"],
      lw["gn_w"], lw["sinks"])


def _gate_out_kernel(x_ref, oc_ref, nw_ref, wg_ref, bg_ref, wbr_ref, wout_ref, h_ref):
    x = x_ref[...]
    xn = _rms(x, nw_ref[...]).astype(BF16)
    m = None
    for b in range(3):
        cols = slice(b * D_MODEL, (b + 1) * D_MODEL)
        gate = jax.nn.sigmoid(jnp.dot(xn, wg_ref[:, cols], preferred_element_type=F32)
                              + bg_ref[:, cols])
        p = jnp.dot(oc_ref[:, b * 512:(b + 1) * 512], wbr_ref[b * 512:(b + 1) * 512, :],
                    preferred_element_type=F32)
        m = gate * p if m is None else m + gate * p
    h_ref[...] = x + jnp.dot(m.astype(BF16), wout_ref[...], preferred_element_type=F32)


def _gate_out(x, ocat, nw, wg, bg, wbr, wout):
    m, d = x.shape
    tm = _tile(m, 512)
    const = lambda i: (0, 0)
    return pl.pallas_call(
        _gate_out_kernel,
        grid=(m // tm,),
        in_specs=[pl.BlockSpec((tm, d), lambda i: (i, 0)),
                  pl.BlockSpec((tm, 3 * 512), lambda i: (i, 0)),
                  pl.BlockSpec((1, d), const),
                  pl.BlockSpec((d, GATE_WIDTH), const),
                  pl.BlockSpec((1, GATE_WIDTH), const),
                  pl.BlockSpec((3 * 512, d), const),
                  pl.BlockSpec((d, d), const)],
        out_specs=pl.BlockSpec((tm, d), lambda i: (i, 0)),
        out_shape=jax.ShapeDtypeStruct((m, d), F32),
        compiler_params=pltpu.CompilerParams(dimension_semantics=("parallel",),
                                             vmem_limit_bytes=VMEM_LIMIT),
        name="gate_out",
    )(x, ocat, nw, wg, bg, wbr, wout)


def _col_max(w):
    return jnp.max(w, axis=0, keepdims=True)


def _top_sorted(w, with_rank):
    row = lax.broadcasted_iota(jnp.int32, (PEER_TOPK, w.shape[1]), 0)
    out = jnp.zeros((PEER_TOPK, w.shape[1]), F32)
    rank = jnp.full(w.shape, float(PEER_TOPK), F32)
    for a in range(PEER_TOPK):
        m = _col_max(w)
        out = jnp.where(row == a, m, out)
        top = w == m
        if with_rank:
            rank = jnp.where(top, float(a), rank)
        if a + 1 < PEER_TOPK:
            w = jnp.where(top, -jnp.inf, w)
    return out, rank


def _select_counts(v1, v2):
    blocks = [v2 + v1[0:1]]
    for a in range(1, 8):
        blocks.append(v2[0:8] + v1[a:a + 1])
    blocks.append(v1[8:16] + v2[0:1])
    cand = jnp.concatenate(blocks, axis=0)
    cell = lax.broadcasted_iota(jnp.int32, cand.shape, 0)
    w = cand
    ones = jnp.zeros_like(cand)
    for _ in range(PEER_TOPK):
        first = jnp.min(jnp.where(w == _col_max(w), cell, cand.shape[0]), axis=0, keepdims=True)
        pick = cell == first
        ones = jnp.where(pick, 1.0, ones)
        w = jnp.where(pick, -jnp.inf, w)
    e = jnp.where(ones > 0.0, jnp.exp(cand - (v1[0:1] + v2[0:1])), 0.0)
    row = lax.broadcasted_iota(jnp.int32, (8, v1.shape[1]), 0)
    low = jnp.zeros((8, v1.shape[1]), F32)
    for a in range(8):
        lo = 0 if a == 0 else 8 + 8 * a
        hi = 16 if a == 0 else lo + 8
        low = jnp.where(row == a, jnp.sum(ones[lo:hi], axis=0, keepdims=True), low)
    counts = jnp.concatenate([low, ones[72:80]], axis=0)
    return counts, 1.0 / jnp.sum(e, axis=0, keepdims=True)


def _peer_kernel(*refs, final_norm):
    if final_norm:
        (h_ref, nw_ref, wqt_ref, k1_ref, k2_ref, pu_ref, pvt_ref, nf_ref, out_ref,
         xnt_scr, cnt_scr, a_scr, rank_scr, b_scr, z_scr, acc_scr) = refs
    else:
        (h_ref, nw_ref, wqt_ref, k1_ref, k2_ref, pu_ref, pvt_ref, out_ref,
         xnt_scr, cnt_scr, a_scr, rank_scr, b_scr, z_scr, acc_scr) = refs
        nf_ref = None
    j = pl.program_id(1)
    tm = h_ref.shape[0]
    nlc = tm // LANES
    nsub = pu_ref.shape[0] // PEER_NKEYS
    dq = 2 * PEER_NKEYS
    quarter = PEER_QUARTER_KEYS * PEER_NKEYS

    @pl.when(j == 0)
    def _prepare():
        xn = _rms(h_ref[...], nw_ref[...])
        xnt = xn.T.astype(BF16)
        xnt_scr[...] = xnt
        qt = jnp.dot(wqt_ref[...], xnt, preferred_element_type=F32).astype(BF16)
        for h in range(PEER_HEADS):
            s1 = jnp.dot(k1_ref[...], qt[h * dq:h * dq + PEER_NKEYS], preferred_element_type=F32)
            s2 = jnp.dot(k2_ref[...], qt[h * dq + PEER_NKEYS:(h + 1) * dq],
                         preferred_element_type=F32)
            for lc in range(nlc):
                cs = slice(lc * LANES, (lc + 1) * LANES)
                s1c = s1[:, cs]
                s2c = s2[:, cs]
                v1, _ = _top_sorted(s1c, False)
                v2, rank2 = _top_sorted(s2c, True)
                counts, zinv = _select_counts(v1, v2)
                cnt = jnp.zeros_like(s1c)
                for a in range(PEER_TOPK):
                    cnt = jnp.where(s1c == v1[a:a + 1], counts[a:a + 1], cnt)
                cnt_scr[h, :, cs] = cnt
                a_scr[h, :, cs] = jnp.exp(s1c - v1[0:1])
                rank_scr[h, :, cs] = rank2.astype(BF16)
                b_scr[h, :, cs] = (jnp.exp(s2c - v2[0:1]) * zinv).astype(BF16)
        acc_scr[...] = jnp.zeros_like(acc_scr)

    base = pl.multiple_of(j * nsub, nsub)

    def activations(q):
        return jnp.dot(pu_ref[q * quarter:(q + 1) * quarter, :], xnt_scr[...],
                       preferred_element_type=F32)

    def weigh(act, q):
        for lc in range(nlc):
            cs = slice(lc * LANES, (lc + 1) * LANES)
            crows = [cnt_scr[h, pl.ds(base, nsub), cs] for h in range(PEER_HEADS)]
            arows = [a_scr[h, pl.ds(base, nsub), cs] for h in range(PEER_HEADS)]
            for ii in range(PEER_QUARTER_KEYS):
                r = q * PEER_QUARTER_KEYS + ii
                w = jnp.zeros((PEER_NKEYS, LANES), BF16)
                for h in range(PEER_HEADS):
                    hit = rank_scr[h, :, cs] < crows[h][r:r + 1].astype(BF16)
                    w = w + (jnp.where(hit, b_scr[h, :, cs], jnp.zeros((), BF16))
                             * arows[h][r:r + 1].astype(BF16))
                g = _gelu(act[ii * PEER_NKEYS:(ii + 1) * PEER_NKEYS, cs]).astype(BF16)
                z_scr[r * PEER_NKEYS:(r + 1) * PEER_NKEYS, cs] = w * g

    for q in range(nsub // PEER_QUARTER_KEYS):
        weigh(activations(q), q)
    acc_scr[...] += jnp.dot(pvt_ref[...], z_scr[...], preferred_element_type=F32)

    @pl.when(j == pl.num_programs(1) - 1)
    def _finish():
        y = h_ref[...] + acc_scr[...].T
        if final_norm:
            y = _rms(y, nf_ref[...])
        out_ref[...] = y


def _peer(h, nw, wqt, k1, k2, pu, pvt, nf):
    m, d = h.shape
    ne = pu.shape[0]
    tm = _tile(m, 512)
    eb = PEER_EXPERT_BLOCK
    assert eb == 8 * PEER_NKEYS and ne % eb == 0
    const = lambda i, j: (0, 0)
    in_specs = [pl.BlockSpec((tm, d), lambda i, j: (i, 0)),
                pl.BlockSpec((1, d), const),
                pl.BlockSpec(wqt.shape, const),
                pl.BlockSpec(k1.shape, const),
                pl.BlockSpec(k2.shape, const),
                pl.BlockSpec((eb, d), lambda i, j: (j, 0)),
                pl.BlockSpec((d, eb), lambda i, j: (0, j))]
    args = [h, nw, wqt, k1, k2, pu, pvt]
    if nf is not None:
        in_specs.append(pl.BlockSpec((1, d), const))
        args.append(nf)
    sel = pltpu.VMEM((PEER_HEADS, PEER_NKEYS, tm), F32)
    sel16 = pltpu.VMEM((PEER_HEADS, PEER_NKEYS, tm), BF16)
    return pl.pallas_call(
        functools.partial(_peer_kernel, final_norm=nf is not None),
        grid=(m // tm, ne // eb),
        in_specs=in_specs,
        out_specs=pl.BlockSpec((tm, d), lambda i, j: (i, 0)),
        out_shape=jax.ShapeDtypeStruct((m, d), F32),
        scratch_shapes=[pltpu.VMEM((d, tm), BF16), sel, sel, sel16, sel16,
                        pltpu.VMEM((eb, tm), BF16),
                        pltpu.VMEM((d, tm), F32)],
        compiler_params=pltpu.CompilerParams(dimension_semantics=("parallel", "arbitrary"),
                                             vmem_limit_bytes=VMEM_LIMIT),
        name="peer",
    )(*args)


def _rope_tables(pos):
    half = HEAD_DIM // 2
    inv = jnp.exp(-math.log(ROPE_THETA) * jnp.arange(half, dtype=F32) / half)
    ang = pos.astype(F32)[:, None] * inv[None, :]
    cos, sin = jnp.cos(ang), jnp.sin(ang)
    return (jnp.tile(jnp.concatenate([cos, cos], axis=-1), (1, RET_HEADS)),
            jnp.tile(jnp.concatenate([-sin, sin], axis=-1), (1, RET_HEADS)))


def _decay_tables(t_idx, same_seq, c_len):
    lg = jnp.log1p(-jnp.exp2(-5.0 - jnp.arange(RET_HEADS, dtype=F32)))[:, None]
    t = t_idx.astype(F32)
    diff = t[:, None] - t[None, :]
    dmat = jnp.where((diff[None] >= 0) & same_seq[None],
                     jnp.exp(jnp.maximum(diff, 0.0)[None] * lg[:, :, None]), 0.0)
    qdec = jnp.repeat(jnp.exp((t + 1.0)[None] * lg).T, HEAD_DIM, axis=1)
    kdec = jnp.repeat(jnp.exp((c_len - 1.0 - t)[None] * lg).T, HEAD_DIM, axis=1)
    sdec = jnp.broadcast_to(jnp.exp(c_len * lg)[:, :, None], (RET_HEADS, HEAD_DIM, HEAD_DIM))
    return dmat, qdec, kdec, sdec


def kernel(x_prompt, x_sample, state_ret, cache_win_k, cache_win_v, norm1_w, norm2_w, normf_w, w_in, b_gate, gm_ln_w, gm_ln_b, gm_ws, gm_b, ret_gn_w, attn_sinks, w_br_a, w_br_b, w_br_c, w_out, peer_wq, peer_k1, peer_k2, peer_u, peer_v):
    batch, seq, d = x_prompt.shape
    ns, t_len, _ = x_sample.shape
    depth = w_in.shape[0]
    wb = cache_win_k.shape[2]
    assert d == D_MODEL and seq % CHUNK == 0 and t_len * SEQ_PER_STEP == CHUNK
    assert ns % SEQ_PER_STEP == 0 and wb == CHUNK
    xp = x_prompt.reshape(batch * seq, d)
    xs = x_sample.reshape(ns * t_len, d)

    row = jnp.arange(CHUNK)
    tabs = {}
    tabs["cos_p"], tabs["sin_p"] = _rope_tables(jnp.arange(seq))
    tabs["cos_s"], tabs["sin_s"] = _rope_tables(PAST_LEN + row % t_len)
    (tabs["dmat_p"], tabs["qdec_p"], tabs["kdec_p"], tabs["sdec_p"]) = _decay_tables(
        row, jnp.ones((CHUNK, CHUNK), bool), float(CHUNK))
    (tabs["dmat_s"], tabs["qdec_s"], tabs["kdec_s"], tabs["sdec_s"]) = _decay_tables(
        row % t_len, (row // t_len)[:, None] == (row // t_len)[None, :], float(t_len))
    causal = jnp.tril(jnp.ones((CHUNK, CHUNK), bool))
    eye = jnp.eye(SEQ_PER_STEP, dtype=F32)

    s_p, s_s, k_p, v_p, k_s, v_s, g_s = [], [], [], [], [], [], []
    for l in range(depth):
        ws = gm_ws[l]
        causal_s = jnp.tril(jnp.ones((t_len, t_len), bool))
        ws_s = jnp.where(causal_s[None], ws[:, :t_len, :t_len], 0.0)
        lw = {
            "wm_p": jnp.where(causal[None], ws, 0.0).astype(BF16),
            "gb_p": jnp.broadcast_to(gm_b[l][:, :, None], (GM_GROUPS, CHUNK, CHUNK)),
            "wm_s": jnp.stack([jnp.kron(eye, ws_s[g]) for g in range(GM_GROUPS)]).astype(BF16),
            "gb_s": jnp.broadcast_to(jnp.tile(gm_b[l][:, :t_len], (1, SEQ_PER_STEP))[:, :, None],
                                     (GM_GROUPS, CHUNK, CHUNK)),
            "ln_w": gm_ln_w[l][None], "ln_b": gm_ln_b[l][None],
            "gn_w": ret_gn_w[l][None], "sinks": attn_sinks[l],
        }
        w_in_l = w_in[l].astype(BF16)
        n1 = norm1_w[l][None]
        proj_p = _in_proj(xp, n1, w_in_l[:, :MIX_WIDTH])
        proj_s = _in_proj(xs, n1, w_in_l[:, :MIX_WIDTH])
        ocat_p, sp, kp, vp = _prompt_mixer(proj_p, batch, seq, tabs, lw)
        ocat_s, ss, ks, vs, gs = _sample_mixer(
            proj_s, state_ret[l], cache_win_k[l].reshape(ns, wb, 128),
            cache_win_v[l].reshape(ns, wb, 128), tabs, lw)
        gate_w = (n1, w_in_l[:, MIX_WIDTH:], b_gate[l][None],
                  jnp.concatenate([w_br_a[l], w_br_b[l], w_br_c[l]], axis=0).astype(BF16),
                  w_out[l].astype(BF16))
        peer_w = (norm2_w[l][None], peer_wq[l].T.astype(BF16),
                  peer_k1[l].astype(BF16), peer_k2[l].astype(BF16),
                  peer_u[l].astype(BF16), peer_v[l].T.astype(BF16),
                  normf_w[None] if l == depth - 1 else None)
        xp = _peer(_gate_out(xp, ocat_p, *gate_w), *peer_w)
        xs = _peer(_gate_out(xs, ocat_s, *gate_w), *peer_w)
        s_p.append(sp)
        s_s.append(ss)
        k_p.append(kp.reshape(batch, CHUNK, ATT_KV_HEADS, HEAD_DIM))
        v_p.append(vp.reshape(batch, CHUNK, ATT_KV_HEADS, HEAD_DIM))
        k_s.append(ks.reshape(ns, wb, ATT_KV_HEADS, HEAD_DIM))
        v_s.append(vs.reshape(ns, wb, ATT_KV_HEADS, HEAD_DIM))
        g_s.append(gs.reshape(ns, t_len, GM_WIDTH))
    return (xp.reshape(batch, seq, d), xs.reshape(ns, t_len, d),
            jnp.stack(s_p), jnp.stack(s_s), jnp.stack(k_p), jnp.stack(v_p),
            jnp.stack(k_s), jnp.stack(v_s), jnp.stack(g_s))
```

```python
import functools
import math

import numpy as np
import jax
import jax.numpy as jnp
from jax import lax
from jax.experimental import pallas as pl
from jax.experimental.pallas import tpu as pltpu

F32 = jnp.float32
BF16 = jnp.bfloat16

D_MODEL = 1024
HEAD_DIM = 64
CHUNK = 128
GM_WIDTH = 512
GM_GROUPS = 4
RET_HEADS = 8
ATT_Q_HEADS = 8
ATT_KV_HEADS = 2
ATT_REP = 4
ROPE_THETA = 10000.0
PAST_LEN = 16384
PEER_HEADS = 8
PEER_NKEYS = 128
PEER_TOPK = 16
EPS = 1e-6
NEG = -1e30
MIX_WIDTH = 3840
GATE_WIDTH = 3 * D_MODEL
SEQ_PER_STEP = 16
LANES = 128
PEER_EXPERT_BLOCK = 1024
PEER_QUARTER_KEYS = 2
VMEM_LIMIT = 56 * 1024 * 1024


def _tile(m, target):
    t = min(m, target)
    while m % t or t % LANES:
        t -= LANES
    return t


def _rms(x, w):
    return x * lax.rsqrt(jnp.mean(x * x, axis=-1, keepdims=True) + EPS) * w


def _gelu(x):
    return 0.5 * x * (1.0 + lax.erf(x * np.float32(math.sqrt(0.5))))


def _rope(x, cos, sin_signed):
    w = x.shape[-1]
    lane = lax.broadcasted_iota(jnp.int32, x.shape, 1)
    first = (lane & (HEAD_DIM - 1)) < HEAD_DIM // 2
    rot = jnp.where(first, pltpu.roll(x, w - HEAD_DIM // 2, 1), pltpu.roll(x, HEAD_DIM // 2, 1))
    return x * cos + rot * sin_signed


def _head(x, h):
    return x[:, h * HEAD_DIM:(h + 1) * HEAD_DIM]


def _gmlp(gu, gv, lnw, lnb, wm_ref, gb_ref):
    u = _gelu(gu)
    vf = _gelu(gv)
    mu = jnp.mean(vf, axis=-1, keepdims=True)
    var = jnp.mean(jnp.square(vf - mu), axis=-1, keepdims=True)
    vn = (vf - mu) * lax.rsqrt(var + EPS) * lnw + lnb
    vnb = vn.astype(BF16)
    parts = []
    for g in range(GM_GROUPS):
        s = jnp.dot(wm_ref[g], vnb[:, g * LANES:(g + 1) * LANES], preferred_element_type=F32)
        parts.append(s + gb_ref[g])
    return u * jnp.concatenate(parts, axis=-1), vn


def _group_norm(o):
    mu = jnp.mean(o, axis=-1, keepdims=True)
    var = jnp.mean(jnp.square(o - mu), axis=-1, keepdims=True)
    return (o - mu) * lax.rsqrt(var + EPS)


def _dot_nt(a, b):
    return lax.dot_general(a, b, (((1,), (1,)), ((), ())), preferred_element_type=F32)


def _in_proj_kernel(x_ref, nw_ref, w_ref, o_ref):
    xn = _rms(x_ref[...], nw_ref[...]).astype(BF16)
    o_ref[...] = jnp.dot(xn, w_ref[...], preferred_element_type=F32)


def _in_proj(x, nw, w):
    m, d = x.shape
    n = w.shape[1]
    tm = _tile(m, 512)
    return pl.pallas_call(
        _in_proj_kernel,
        grid=(m // tm,),
        in_specs=[pl.BlockSpec((tm, d), lambda i: (i, 0)),
                  pl.BlockSpec((1, d), lambda i: (0, 0)),
                  pl.BlockSpec((d, n), lambda i: (0, 0))],
        out_specs=pl.BlockSpec((tm, n), lambda i: (i, 0)),
        out_shape=jax.ShapeDtypeStruct((m, n), F32),
        compiler_params=pltpu.CompilerParams(dimension_semantics=("parallel",),
                                             vmem_limit_bytes=VMEM_LIMIT),
        name="in_proj",
    )(x, nw, w)


def _prompt_mixer_kernel(proj_ref, cos_ref, sin_ref, wm_ref, gb_ref, lnw_ref, lnb_ref,
                         dmat_ref, qdec_ref, kdec_ref, sdec_ref, gnw_ref, sinks_ref,
                         ocat_ref, sfin_ref, kout_ref, vout_ref,
                         s_scr, kprev_scr, vprev_scr):
    c = pl.program_id(1)

    @pl.when(c == 0)
    def _reset():
        s_scr[...] = jnp.zeros_like(s_scr)
        kprev_scr[...] = jnp.zeros_like(kprev_scr)
        vprev_scr[...] = jnp.zeros_like(vprev_scr)

    cos = cos_ref[...]
    sin = sin_ref[...]

    o_a, _ = _gmlp(proj_ref[:, 0:512], proj_ref[:, 512:1024], lnw_ref[...], lnb_ref[...],
                   wm_ref, gb_ref)
    ocat_ref[:, 0:512] = o_a.astype(BF16)

    q = _rope(proj_ref[:, 1024:1536], cos, sin)
    ks = _rope(proj_ref[:, 1536:2048], cos, sin) * (HEAD_DIM ** -0.5)
    v = proj_ref[:, 2048:2560]
    qd = q * qdec_ref[...]
    kd = ks * kdec_ref[...]
    ys = []
    for h in range(RET_HEADS):
        qh = _head(q, h).astype(BF16)
        kh = _head(ks, h).astype(BF16)
        vh = _head(v, h).astype(BF16)
        sc = _dot_nt(qh, kh) * dmat_ref[h]
        o = jnp.dot(sc.astype(BF16), vh, preferred_element_type=F32)
        s_old = s_scr[h]
        o = o + jnp.dot(_head(qd, h).astype(BF16), s_old.astype(BF16), preferred_element_type=F32)
        kdh = _head(kd, h).astype(BF16)
        s_new = s_old * sdec_ref[h] + lax.dot_general(
            kdh, vh, (((0,), (0,)), ((), ())), preferred_element_type=F32)
        s_scr[h] = s_new
        sfin_ref[h] = s_new
        ys.append(_group_norm(o))
    y = jnp.concatenate(ys, axis=-1) * gnw_ref[...]
    rg = proj_ref[:, 2560:3072]
    ocat_ref[:, 512:1024] = (rg * jax.nn.sigmoid(rg) * y).astype(BF16)

    qa = _rope(proj_ref[:, 3072:3584], cos, sin)
    ka = _rope(proj_ref[:, 3584:3712], cos[:, 0:128], sin[:, 0:128])
    va = proj_ref[:, 3712:3840]
    kout_ref[...] = ka
    vout_ref[...] = va
    rows = ATT_REP * CHUNK
    tt = lax.broadcasted_iota(jnp.int32, (rows, 2 * CHUNK), 0) & (CHUNK - 1)
    kk = lax.broadcasted_iota(jnp.int32, (rows, 2 * CHUNK), 1)
    valid = (kk > tt) & (kk <= tt + CHUNK) & ((kk >= CHUNK) | (c > 0))
    rr = lax.broadcasted_iota(jnp.int32, (rows, 1), 0) // CHUNK
    outs = [None] * ATT_Q_HEADS
    for g in range(ATT_KV_HEADS):
        kcat = jnp.concatenate([_head(kprev_scr[...], g), _head(ka, g)], axis=0).astype(BF16)
        vcat = jnp.concatenate([_head(vprev_scr[...], g), _head(va, g)], axis=0).astype(BF16)
        qg = jnp.concatenate([_head(qa, g * ATT_REP + r) for r in range(ATT_REP)],
                             axis=0).astype(BF16)
        s = _dot_nt(qg, kcat) * (HEAD_DIM ** -0.5)
        s = jnp.where(valid, s, NEG)
        sink = jnp.zeros((rows, 1), F32)
        for r in range(ATT_REP):
            sink = jnp.where(rr == r, sinks_ref[g * ATT_REP + r], sink)
        mx = jnp.maximum(jnp.max(s, axis=-1, keepdims=True), sink)
        e = jnp.exp(s - mx)
        den = jnp.sum(e, axis=-1, keepdims=True) + jnp.exp(sink - mx)
        p = (e / den).astype(BF16)
        o = jnp.dot(p, vcat, preferred_element_type=F32)
        for r in range(ATT_REP):
            outs[g * ATT_REP + r] = o[r * CHUNK:(r + 1) * CHUNK]
    ocat_ref[:, 1024:1536] = jnp.concatenate(outs, axis=-1).astype(BF16)
    kprev_scr[...] = ka
    vprev_scr[...] = va


def _prompt_mixer(proj, batch, seq, tabs, lw):
    nchunk = seq // CHUNK
    const2 = lambda b, c: (0, 0)
    const3 = lambda b, c: (0, 0, 0)
    row_blk = lambda b, c: (b * nchunk + c, 0)
    return pl.pallas_call(
        _prompt_mixer_kernel,
        grid=(batch, nchunk),
        in_specs=[pl.BlockSpec((CHUNK, MIX_WIDTH), row_blk),
                  pl.BlockSpec((CHUNK, 512), lambda b, c: (c, 0)),
                  pl.BlockSpec((CHUNK, 512), lambda b, c: (c, 0)),
                  pl.BlockSpec((GM_GROUPS, CHUNK, CHUNK), const3),
                  pl.BlockSpec((GM_GROUPS, CHUNK, CHUNK), const3),
                  pl.BlockSpec((1, 512), const2),
                  pl.BlockSpec((1, 512), const2),
                  pl.BlockSpec((RET_HEADS, CHUNK, CHUNK), const3),
                  pl.BlockSpec((CHUNK, 512), const2),
                  pl.BlockSpec((CHUNK, 512), const2),
                  pl.BlockSpec((RET_HEADS, HEAD_DIM, HEAD_DIM), const3),
                  pl.BlockSpec((1, 512), const2),
                  pl.BlockSpec(memory_space=pltpu.SMEM)],
        out_specs=[pl.BlockSpec((CHUNK, 3 * 512), row_blk),
                   pl.BlockSpec((None, RET_HEADS, HEAD_DIM, HEAD_DIM), lambda b, c: (b, 0, 0, 0)),
                   pl.BlockSpec((None, CHUNK, 128), lambda b, c: (b, 0, 0)),
                   pl.BlockSpec((None, CHUNK, 128), lambda b, c: (b, 0, 0))],
        out_shape=[jax.ShapeDtypeStruct((batch * seq, 3 * 512), BF16),
                   jax.ShapeDtypeStruct((batch, RET_HEADS, HEAD_DIM, HEAD_DIM), F32),
                   jax.ShapeDtypeStruct((batch, CHUNK, 128), F32),
                   jax.ShapeDtypeStruct((batch, CHUNK, 128), F32)],
        scratch_shapes=[pltpu.VMEM((RET_HEADS, HEAD_DIM, HEAD_DIM), F32),
                        pltpu.VMEM((CHUNK, 128), F32),
                        pltpu.VMEM((CHUNK, 128), F32)],
        compiler_params=pltpu.CompilerParams(dimension_semantics=("parallel", "arbitrary"),
                                             vmem_limit_bytes=VMEM_LIMIT),
        name="prompt_mixer",
    )(proj, tabs["cos_p"], tabs["sin_p"], lw["wm_p"], lw["gb_p"], lw["ln_w"], lw["ln_b"],
      tabs["dmat_p"], tabs["qdec_p"], tabs["kdec_p"], tabs["sdec_p"], lw["gn_w"], lw["sinks"])


def _sample_mixer_kernel(proj_ref, s0_ref, ck_ref, cv_ref, cos_ref, sin_ref,
                         wm_ref, gb_ref, lnw_ref, lnb_ref, dmat_ref, qdec_ref, kdec_ref,
                         sdec_ref, gnw_ref, sinks_ref,
                         ocat_ref, snew_ref, kout_ref, vout_ref, vn_ref):
    nb = s0_ref.shape[0]
    t_len = proj_ref.shape[0] // nb
    cos = cos_ref[...]
    sin = sin_ref[...]

    o_a, vn = _gmlp(proj_ref[:, 0:512], proj_ref[:, 512:1024], lnw_ref[...], lnb_ref[...],
                    wm_ref, gb_ref)
    ocat_ref[:, 0:512] = o_a.astype(BF16)
    vn_ref[...] = vn

    def seq3(x):
        return x.reshape(nb, t_len, x.shape[-1])

    q = _rope(proj_ref[:, 1024:1536], cos, sin)
    ks = _rope(proj_ref[:, 1536:2048], cos, sin) * (HEAD_DIM ** -0.5)
    v = proj_ref[:, 2048:2560]
    qd = q * qdec_ref[...]
    kd = ks * kdec_ref[...]
    ys = []
    for h in range(RET_HEADS):
        qh = _head(q, h).astype(BF16)
        kh = _head(ks, h).astype(BF16)
        vh = _head(v, h).astype(BF16)
        sc = _dot_nt(qh, kh) * dmat_ref[h]
        o = jnp.dot(sc.astype(BF16), vh, preferred_element_type=F32)
        s_old = s0_ref[:, h]
        oc = jnp.einsum("ntd,nde->nte", seq3(_head(qd, h)).astype(BF16), s_old.astype(BF16),
                        preferred_element_type=F32)
        o = o + oc.reshape(nb * t_len, HEAD_DIM)
        kd3t = jnp.swapaxes(seq3(_head(kd, h)), 1, 2).astype(BF16)
        upd = jnp.einsum("ndt,nte->nde", kd3t, seq3(_head(v, h)).astype(BF16),
                         preferred_element_type=F32)
        snew_ref[:, h] = s_old * sdec_ref[h] + upd
        ys.append(_group_norm(o))
    y = jnp.concatenate(ys, axis=-1) * gnw_ref[...]
    rg = proj_ref[:, 2560:3072]
    ocat_ref[:, 512:1024] = (rg * jax.nn.sigmoid(rg) * y).astype(BF16)

    qa = _rope(proj_ref[:, 3072:3584], cos, sin)
    ka = _rope(proj_ref[:, 3584:3712], cos[:, 0:128], sin[:, 0:128])
    va = proj_ref[:, 3712:3840]
    wb = ck_ref.shape[1]
    kout_ref[:, 0:wb - t_len, :] = ck_ref[:, t_len:wb, :]
    kout_ref[:, wb - t_len:wb, :] = seq3(ka)
    vout_ref[:, 0:wb - t_len, :] = cv_ref[:, t_len:wb, :]
    vout_ref[:, wb - t_len:wb, :] = seq3(va)
    nq = ATT_REP * t_len
    tq_c = lax.broadcasted_iota(jnp.int32, (nb, nq, wb), 1) & (t_len - 1)
    kk_c = lax.broadcasted_iota(jnp.int32, (nb, nq, wb), 2)
    valid_c = kk_c > tq_c
    tq_n = lax.broadcasted_iota(jnp.int32, (nb, nq, t_len), 1) & (t_len - 1)
    kk_n = lax.broadcasted_iota(jnp.int32, (nb, nq, t_len), 2)
    valid_n = kk_n <= tq_n
    rr = lax.broadcasted_iota(jnp.int32, (1, nq, 1), 1) // t_len
    outs = [None] * ATT_Q_HEADS
    for g in range(ATT_KV_HEADS):
        gs = slice(g * HEAD_DIM, (g + 1) * HEAD_DIM)
        kc = ck_ref[:, :, gs].astype(BF16)
        vc = cv_ref[:, :, gs].astype(BF16)
        kn = seq3(_head(ka, g)).astype(BF16)
        vnw = seq3(_head(va, g)).astype(BF16)
        qg = jnp.concatenate([seq3(_head(qa, g * ATT_REP + r)) for r in range(ATT_REP)],
                             axis=1).astype(BF16)
        s_c = jnp.einsum("nqd,nkd->nqk", qg, kc, preferred_element_type=F32) * (HEAD_DIM ** -0.5)
        s_n = jnp.einsum("nqd,nkd->nqk", qg, kn, preferred_element_type=F32) * (HEAD_DIM ** -0.5)
        s_c = jnp.where(valid_c, s_c, NEG)
        s_n = jnp.where(valid_n, s_n, NEG)
        sink = jnp.zeros((1, nq, 1), F32)
        for r in range(ATT_REP):
            sink = jnp.where(rr == r, sinks_ref[g * ATT_REP + r], sink)
        mx = jnp.maximum(jnp.maximum(jnp.max(s_c, axis=-1, keepdims=True),
                                     jnp.max(s_n, axis=-1, keepdims=True)), sink)
        e_c = jnp.exp(s_c - mx)
        e_n = jnp.exp(s_n - mx)
        den = (jnp.sum(e_c, axis=-1, keepdims=True) + jnp.sum(e_n, axis=-1, keepdims=True)
               + jnp.exp(sink - mx))
        o = (jnp.einsum("nqk,nkd->nqd", (e_c / den).astype(BF16), vc, preferred_element_type=F32)
             + jnp.einsum("nqk,nkd->nqd", (e_n / den).astype(BF16), vnw,
                          preferred_element_type=F32))
        for r in range(ATT_REP):
            outs[g * ATT_REP + r] = o[:, r * t_len:(r + 1) * t_len, :].reshape(nb * t_len, HEAD_DIM)
    ocat_ref[:, 1024:1536] = jnp.concatenate(outs, axis=-1).astype(BF16)


def _sample_mixer(proj, s0, ck, cv, tabs, lw):
    ns, wb = ck.shape[0], ck.shape[1]
    nb = SEQ_PER_STEP
    t_len = CHUNK // nb
    const2 = lambda i: (0, 0)
    const3 = lambda i: (0, 0, 0)
    row_blk = lambda i: (i, 0)
    seq_blk3 = lambda i: (i, 0, 0)
    return pl.pallas_call(
        _sample_mixer_kernel,
        grid=(ns // nb,),
        in_specs=[pl.BlockSpec((CHUNK, MIX_WIDTH), row_blk),
                  pl.BlockSpec((nb, RET_HEADS, HEAD_DIM, HEAD_DIM), lambda i: (i, 0, 0, 0)),
                  pl.BlockSpec((nb, wb, 128), seq_blk3),
                  pl.BlockSpec((nb, wb, 128), seq_blk3),
                  pl.BlockSpec((CHUNK, 512), const2),
                  pl.BlockSpec((CHUNK, 512), const2),
                  pl.BlockSpec((GM_GROUPS, CHUNK, CHUNK), const3),
                  pl.BlockSpec((GM_GROUPS, CHUNK, CHUNK), const3),
                  pl.BlockSpec((1, 512), const2),
                  pl.BlockSpec((1, 512), const2),
                  pl.BlockSpec((RET_HEADS, CHUNK, CHUNK), const3),
                  pl.BlockSpec((CHUNK, 512), const2),
                  pl.BlockSpec((CHUNK, 512), const2),
                  pl.BlockSpec((RET_HEADS, HEAD_DIM, HEAD_DIM), const3),
                  pl.BlockSpec((1, 512), const2),
                  pl.BlockSpec(memory_space=pltpu.SMEM)],
        out_specs=[pl.BlockSpec((CHUNK, 3 * 512), row_blk),
                   pl.BlockSpec((nb, RET_HEADS, HEAD_DIM, HEAD_DIM), lambda i: (i, 0, 0, 0)),
                   pl.BlockSpec((nb, wb, 128), seq_blk3),
                   pl.BlockSpec((nb, wb, 128), seq_blk3),
                   pl.BlockSpec((CHUNK, GM_WIDTH), lambda i: (i, 0))],
        out_shape=[jax.ShapeDtypeStruct((ns * t_len, 3 * 512), BF16),
                   jax.ShapeDtypeStruct(s0.shape, F32),
                   jax.ShapeDtypeStruct(ck.shape, F32),
                   jax.ShapeDtypeStruct(cv.shape, F32),
                   jax.ShapeDtypeStruct((ns * t_len, GM_WIDTH), F32)],
        compiler_params=pltpu.CompilerParams(dimension_semantics=("parallel",),
                                             vmem_limit_bytes=VMEM_LIMIT),
        name="sample_mixer",
    )(proj, s0, ck, cv, tabs["cos_s"], tabs["sin_s"], lw["wm_s"], lw["gb_s"],
      lw["ln_w"], lw["ln_b"], tabs["dmat_s"], tabs["qdec_s"], tabs["kdec_s"], tabs["sdec_s"],
      lw["gn_w"], lw["sinks"])


def _gate_out_kernel(x_ref, oc_ref, nw_ref, wg_ref, bg_ref, wbr_ref, wout_ref, h_ref):
    x = x_ref[...]
    xn = _rms(x, nw_ref[...]).astype(BF16)
    m = None
    for b in range(3):
        cols = slice(b * D_MODEL, (b + 1) * D_MODEL)
        gate = jax.nn.sigmoid(jnp.dot(xn, wg_ref[:, cols], preferred_element_type=F32)
                              + bg_ref[:, cols])
        p = jnp.dot(oc_ref[:, b * 512:(b + 1) * 512], wbr_ref[b * 512:(b + 1) * 512, :],
                    preferred_element_type=F32)
        m = gate * p if m is None else m + gate * p
    h_ref[...] = x + jnp.dot(m.astype(BF16), wout_ref[...], preferred_element_type=F32)


def _gate_out(x, ocat, nw, wg, bg, wbr, wout):
    m, d = x.shape
    tm = _tile(m, 512)
    const = lambda i: (0, 0)
    return pl.pallas_call(
        _gate_out_kernel,
        grid=(m // tm,),
        in_specs=[pl.BlockSpec((tm, d), lambda i: (i, 0)),
                  pl.BlockSpec((tm, 3 * 512), lambda i: (i, 0)),
                  pl.BlockSpec((1, d), const),
                  pl.BlockSpec((d, GATE_WIDTH), const),
                  pl.BlockSpec((1, GATE_WIDTH), const),
                  pl.BlockSpec((3 * 512, d), const),
                  pl.BlockSpec((d, d), const)],
        out_specs=pl.BlockSpec((tm, d), lambda i: (i, 0)),
        out_shape=jax.ShapeDtypeStruct((m, d), F32),
        compiler_params=pltpu.CompilerParams(dimension_semantics=("parallel",),
                                             vmem_limit_bytes=VMEM_LIMIT),
        name="gate_out",
    )(x, ocat, nw, wg, bg, wbr, wout)


def _col_max(w):
    return jnp.max(w, axis=0, keepdims=True)


def _top_sorted(w, with_rank):
    row = lax.broadcasted_iota(jnp.int32, (PEER_TOPK, w.shape[1]), 0)
    out = jnp.zeros((PEER_TOPK, w.shape[1]), F32)
    rank = jnp.full(w.shape, float(PEER_TOPK), F32)
    for a in range(PEER_TOPK):
        m = _col_max(w)
        out = jnp.where(row == a, m, out)
        top = w == m
        if with_rank:
            rank = jnp.where(top, float(a), rank)
        if a + 1 < PEER_TOPK:
            w = jnp.where(top, -jnp.inf, w)
    return out, rank


def _select_counts(v1, v2):
    blocks = [v2 + v1[0:1]]
    for a in range(1, 8):
        blocks.append(v2[0:8] + v1[a:a + 1])
    blocks.append(v1[8:16] + v2[0:1])
    cand = jnp.concatenate(blocks, axis=0)
    cell = lax.broadcasted_iota(jnp.int32, cand.shape, 0)
    w = cand
    ones = jnp.zeros_like(cand)
    for _ in range(PEER_TOPK):
        first = jnp.min(jnp.where(w == _col_max(w), cell, cand.shape[0]), axis=0, keepdims=True)
        pick = cell == first
        ones = jnp.where(pick, 1.0, ones)
        w = jnp.where(pick, -jnp.inf, w)
    e = jnp.where(ones > 0.0, jnp.exp(cand - (v1[0:1] + v2[0:1])), 0.0)
    row = lax.broadcasted_iota(jnp.int32, (8, v1.shape[1]), 0)
    low = jnp.zeros((8, v1.shape[1]), F32)
    for a in range(8):
        lo = 0 if a == 0 else 8 + 8 * a
        hi = 16 if a == 0 else lo + 8
        low = jnp.where(row == a, jnp.sum(ones[lo:hi], axis=0, keepdims=True), low)
    counts = jnp.concatenate([low, ones[72:80]], axis=0)
    return counts, 1.0 / jnp.sum(e, axis=0, keepdims=True)


def _peer_kernel(*refs, final_norm):
    if final_norm:
        (h_ref, nw_ref, wqt_ref, k1_ref, k2_ref, pu_ref, pvt_ref, nf_ref, out_ref,
         xnt_scr, cnt_scr, a_scr, rank_scr, b_scr, z_scr, acc_scr) = refs
    else:
        (h_ref, nw_ref, wqt_ref, k1_ref, k2_ref, pu_ref, pvt_ref, out_ref,
         xnt_scr, cnt_scr, a_scr, rank_scr, b_scr, z_scr, acc_scr) = refs
        nf_ref = None
    j = pl.program_id(1)
    tm = h_ref.shape[0]
    nlc = tm // LANES
    nsub = pu_ref.shape[0] // PEER_NKEYS
    dq = 2 * PEER_NKEYS
    quarter = PEER_QUARTER_KEYS * PEER_NKEYS

    @pl.when(j == 0)
    def _prepare():
        xn = _rms(h_ref[...], nw_ref[...])
        xnt = xn.T.astype(BF16)
        xnt_scr[...] = xnt
        qt = jnp.dot(wqt_ref[...], xnt, preferred_element_type=F32).astype(BF16)
        for h in range(PEER_HEADS):
            s1 = jnp.dot(k1_ref[...], qt[h * dq:h * dq + PEER_NKEYS], preferred_element_type=F32)
            s2 = jnp.dot(k2_ref[...], qt[h * dq + PEER_NKEYS:(h + 1) * dq],
                         preferred_element_type=F32)
            for lc in range(nlc):
                cs = slice(lc * LANES, (lc + 1) * LANES)
                s1c = s1[:, cs]
                s2c = s2[:, cs]
                v1, _ = _top_sorted(s1c, False)
                v2, rank2 = _top_sorted(s2c, True)
                counts, zinv = _select_counts(v1, v2)
                cnt = jnp.zeros_like(s1c)
                for a in range(PEER_TOPK):
                    cnt = jnp.where(s1c == v1[a:a + 1], counts[a:a + 1], cnt)
                cnt_scr[lc, pl.ds(h, PEER_NKEYS, stride=PEER_HEADS), :] = cnt
                a_scr[lc, pl.ds(h, PEER_NKEYS, stride=PEER_HEADS), :] = jnp.exp(s1c - v1[0:1])
                rank_scr[h, :, cs] = rank2
                b_scr[h, :, cs] = jnp.exp(s2c - v2[0:1]) * zinv
        acc_scr[...] = jnp.zeros_like(acc_scr)

    base = pl.multiple_of(j * nsub, nsub)

    def activations(q):
        return jnp.dot(pu_ref[q * quarter:(q + 1) * quarter, :], xnt_scr[...],
                       preferred_element_type=F32)

    def weigh(act, q):
        for lc in range(nlc):
            cs = slice(lc * LANES, (lc + 1) * LANES)
            keys = [q * PEER_QUARTER_KEYS + ii for ii in range(PEER_QUARTER_KEYS)]
            heads = [pl.ds(pl.multiple_of((base + r) * PEER_HEADS, PEER_HEADS), PEER_HEADS) for r in keys]
            crows = [cnt_scr[lc, hs, :] for hs in heads]
            arows = [a_scr[lc, hs, :] for hs in heads]
            ws = [jnp.zeros((PEER_NKEYS, LANES), F32) for _ in keys]
            for h in range(PEER_HEADS):
                rank = rank_scr[h, :, cs]
                b = b_scr[h, :, cs]
                for n, r in enumerate(keys):
                    hit = rank < crows[n][h:h + 1]
                    ws[n] = ws[n] + jnp.where(hit, b, 0.0) * arows[n][h:h + 1]
            for n, r in enumerate(keys):
                g = _gelu(act[n * PEER_NKEYS:(n + 1) * PEER_NKEYS, cs])
                z_scr[r * PEER_NKEYS:(r + 1) * PEER_NKEYS, cs] = (ws[n] * g).astype(BF16)

    for q in range(nsub // PEER_QUARTER_KEYS):
        weigh(activations(q), q)
    acc_scr[...] += jnp.dot(pvt_ref[...], z_scr[...], preferred_element_type=F32)

    @pl.when(j == pl.num_programs(1) - 1)
    def _finish():
        y = h_ref[...] + acc_scr[...].T
        if final_norm:
            y = _rms(y, nf_ref[...])
        out_ref[...] = y


def _peer(h, nw, wqt, k1, k2, pu, pvt, nf):
    m, d = h.shape
    ne = pu.shape[0]
    tm = _tile(m, 512)
    eb = PEER_EXPERT_BLOCK
    assert eb == 8 * PEER_NKEYS and ne % eb == 0
    const = lambda i, j: (0, 0)
    in_specs = [pl.BlockSpec((tm, d), lambda i, j: (i, 0)),
                pl.BlockSpec((1, d), const),
                pl.BlockSpec(wqt.shape, const),
                pl.BlockSpec(k1.shape, const),
                pl.BlockSpec(k2.shape, const),
                pl.BlockSpec((eb, d), lambda i, j: (j, 0)),
                pl.BlockSpec((d, eb), lambda i, j: (0, j))]
    args = [h, nw, wqt, k1, k2, pu, pvt]
    if nf is not None:
        in_specs.append(pl.BlockSpec((1, d), const))
        args.append(nf)
    sel = pltpu.VMEM((PEER_HEADS, PEER_NKEYS, tm), F32)
    sel_kh = pltpu.VMEM((tm // LANES, PEER_NKEYS * PEER_HEADS, LANES), F32)
    return pl.pallas_call(
        functools.partial(_peer_kernel, final_norm=nf is not None),
        grid=(m // tm, ne // eb),
        in_specs=in_specs,
        out_specs=pl.BlockSpec((tm, d), lambda i, j: (i, 0)),
        out_shape=jax.ShapeDtypeStruct((m, d), F32),
        scratch_shapes=[pltpu.VMEM((d, tm), BF16), sel_kh, sel_kh, sel, sel,
                        pltpu.VMEM((eb, tm), BF16),
                        pltpu.VMEM((d, tm), F32)],
        compiler_params=pltpu.CompilerParams(dimension_semantics=("parallel", "arbitrary"),
                                             vmem_limit_bytes=VMEM_LIMIT),
        name="peer",
    )(*args)


def _rope_tables(pos):
    half = HEAD_DIM // 2
    inv = jnp.exp(-math.log(ROPE_THETA) * jnp.arange(half, dtype=F32) / half)
    ang = pos.astype(F32)[:, None] * inv[None, :]
    cos, sin = jnp.cos(ang), jnp.sin(ang)
    return (jnp.tile(jnp.concatenate([cos, cos], axis=-1), (1, RET_HEADS)),
            jnp.tile(jnp.concatenate([-sin, sin], axis=-1), (1, RET_HEADS)))


def _decay_tables(t_idx, same_seq, c_len):
    lg = jnp.log1p(-jnp.exp2(-5.0 - jnp.arange(RET_HEADS, dtype=F32)))[:, None]
    t = t_idx.astype(F32)
    diff = t[:, None] - t[None, :]
    dmat = jnp.where((diff[None] >= 0) & same_seq[None],
                     jnp.exp(jnp.maximum(diff, 0.0)[None] * lg[:, :, None]), 0.0)
    qdec = jnp.repeat(jnp.exp((t + 1.0)[None] * lg).T, HEAD_DIM, axis=1)
    kdec = jnp.repeat(jnp.exp((c_len - 1.0 - t)[None] * lg).T, HEAD_DIM, axis=1)
    sdec = jnp.broadcast_to(jnp.exp(c_len * lg)[:, :, None], (RET_HEADS, HEAD_DIM, HEAD_DIM))
    return dmat, qdec, kdec, sdec


def kernel(x_prompt, x_sample, state_ret, cache_win_k, cache_win_v, norm1_w, norm2_w, normf_w, w_in, b_gate, gm_ln_w, gm_ln_b, gm_ws, gm_b, ret_gn_w, attn_sinks, w_br_a, w_br_b, w_br_c, w_out, peer_wq, peer_k1, peer_k2, peer_u, peer_v):
    batch, seq, d = x_prompt.shape
    ns, t_len, _ = x_sample.shape
    depth = w_in.shape[0]
    wb = cache_win_k.shape[2]
    assert d == D_MODEL and seq % CHUNK == 0 and t_len * SEQ_PER_STEP == CHUNK
    assert ns % SEQ_PER_STEP == 0 and wb == CHUNK
    xp = x_prompt.reshape(batch * seq, d)
    xs = x_sample.reshape(ns * t_len, d)

    row = jnp.arange(CHUNK)
    tabs = {}
    tabs["cos_p"], tabs["sin_p"] = _rope_tables(jnp.arange(seq))
    tabs["cos_s"], tabs["sin_s"] = _rope_tables(PAST_LEN + row % t_len)
    (tabs["dmat_p"], tabs["qdec_p"], tabs["kdec_p"], tabs["sdec_p"]) = _decay_tables(
        row, jnp.ones((CHUNK, CHUNK), bool), float(CHUNK))
    (tabs["dmat_s"], tabs["qdec_s"], tabs["kdec_s"], tabs["sdec_s"]) = _decay_tables(
        row % t_len, (row // t_len)[:, None] == (row // t_len)[None, :], float(t_len))
    causal = jnp.tril(jnp.ones((CHUNK, CHUNK), bool))
    eye = jnp.eye(SEQ_PER_STEP, dtype=F32)

    s_p, s_s, k_p, v_p, k_s, v_s, g_s = [], [], [], [], [], [], []
    for l in range(depth):
        ws = gm_ws[l]
        causal_s = jnp.tril(jnp.ones((t_len, t_len), bool))
        ws_s = jnp.where(causal_s[None], ws[:, :t_len, :t_len], 0.0)
        lw = {
            "wm_p": jnp.where(causal[None], ws, 0.0).astype(BF16),
            "gb_p": jnp.broadcast_to(gm_b[l][:, :, None], (GM_GROUPS, CHUNK, CHUNK)),
            "wm_s": jnp.stack([jnp.kron(eye, ws_s[g]) for g in range(GM_GROUPS)]).astype(BF16),
            "gb_s": jnp.broadcast_to(jnp.tile(gm_b[l][:, :t_len], (1, SEQ_PER_STEP))[:, :, None],
                                     (GM_GROUPS, CHUNK, CHUNK)),
            "ln_w": gm_ln_w[l][None], "ln_b": gm_ln_b[l][None],
            "gn_w": ret_gn_w[l][None], "sinks": attn_sinks[l],
        }
        w_in_l = w_in[l].astype(BF16)
        n1 = norm1_w[l][None]
        proj_p = _in_proj(xp, n1, w_in_l[:, :MIX_WIDTH])
        proj_s = _in_proj(xs, n1, w_in_l[:, :MIX_WIDTH])
        ocat_p, sp, kp, vp = _prompt_mixer(proj_p, batch, seq, tabs, lw)
        ocat_s, ss, ks, vs, gs = _sample_mixer(
            proj_s, state_ret[l], cache_win_k[l].reshape(ns, wb, 128),
            cache_win_v[l].reshape(ns, wb, 128), tabs, lw)
        gate_w = (n1, w_in_l[:, MIX_WIDTH:], b_gate[l][None],
                  jnp.concatenate([w_br_a[l], w_br_b[l], w_br_c[l]], axis=0).astype(BF16),
                  w_out[l].astype(BF16))
        peer_w = (norm2_w[l][None], peer_wq[l].T.astype(BF16),
                  peer_k1[l].astype(BF16), peer_k2[l].astype(BF16),
                  peer_u[l].astype(BF16), peer_v[l].T.astype(BF16),
                  normf_w[None] if l == depth - 1 else None)
        xp = _peer(_gate_out(xp, ocat_p, *gate_w), *peer_w)
        xs = _peer(_gate_out(xs, ocat_s, *gate_w), *peer_w)
        s_p.append(sp)
        s_s.append(ss)
        k_p.append(kp.reshape(batch, CHUNK, ATT_KV_HEADS, HEAD_DIM))
        v_p.append(vp.reshape(batch, CHUNK, ATT_KV_HEADS, HEAD_DIM))
        k_s.append(ks.reshape(ns, wb, ATT_KV_HEADS, HEAD_DIM))
        v_s.append(vs.reshape(ns, wb, ATT_KV_HEADS, HEAD_DIM))
        g_s.append(gs.reshape(ns, t_len, GM_WIDTH))
    return (xp.reshape(batch, seq, d), xs.reshape(ns, t_len, d),
            jnp.stack(s_p), jnp.stack(s_s), jnp.stack(k_p), jnp.stack(v_p),
            jnp.stack(k_s), jnp.stack(v_s), jnp.stack(g_s))
```

```python
import functools
import math

import numpy as np
import jax
import jax.numpy as jnp
from jax import lax
from jax.experimental import pallas as pl
from jax.experimental.pallas import tpu as pltpu

F32 = jnp.float32
BF16 = jnp.bfloat16

D_MODEL = 1024
HEAD_DIM = 64
CHUNK = 128
GM_WIDTH = 512
GM_GROUPS = 4
RET_HEADS = 8
ATT_Q_HEADS = 8
ATT_KV_HEADS = 2
ATT_REP = 4
ROPE_THETA = 10000.0
PAST_LEN = 16384
PEER_HEADS = 8
PEER_NKEYS = 128
PEER_TOPK = 16
EPS = 1e-6
NEG = -1e30
MIX_WIDTH = 3840
GATE_WIDTH = 3 * D_MODEL
SEQ_PER_STEP = 16
LANES = 128
PEER_EXPERT_BLOCK = 1024
PEER_QUARTER_KEYS = 2
VMEM_LIMIT = 56 * 1024 * 1024


def _tile(m, target):
    t = min(m, target)
    while m % t or t % LANES:
        t -= LANES
    return t


def _rms(x, w):
    return x * lax.rsqrt(jnp.mean(x * x, axis=-1, keepdims=True) + EPS) * w


def _gelu(x):
    return 0.5 * x * (1.0 + lax.erf(x * np.float32(math.sqrt(0.5))))


def _rope(x, cos, sin_signed):
    w = x.shape[-1]
    lane = lax.broadcasted_iota(jnp.int32, x.shape, 1)
    first = (lane & (HEAD_DIM - 1)) < HEAD_DIM // 2
    rot = jnp.where(first, pltpu.roll(x, w - HEAD_DIM // 2, 1), pltpu.roll(x, HEAD_DIM // 2, 1))
    return x * cos + rot * sin_signed


def _head(x, h):
    return x[:, h * HEAD_DIM:(h + 1) * HEAD_DIM]


def _gmlp(gu, gv, lnw, lnb, wm_ref, gb_ref):
    u = _gelu(gu)
    vf = _gelu(gv)
    mu = jnp.mean(vf, axis=-1, keepdims=True)
    var = jnp.mean(jnp.square(vf - mu), axis=-1, keepdims=True)
    vn = (vf - mu) * lax.rsqrt(var + EPS) * lnw + lnb
    vnb = vn.astype(BF16)
    parts = []
    for g in range(GM_GROUPS):
        s = jnp.dot(wm_ref[g], vnb[:, g * LANES:(g + 1) * LANES], preferred_element_type=F32)
        parts.append(s + gb_ref[g])
    return u * jnp.concatenate(parts, axis=-1), vn


def _group_norm(o):
    mu = jnp.mean(o, axis=-1, keepdims=True)
    var = jnp.mean(jnp.square(o - mu), axis=-1, keepdims=True)
    return (o - mu) * lax.rsqrt(var + EPS)


def _dot_nt(a, b):
    return lax.dot_general(a, b, (((1,), (1,)), ((), ())), preferred_element_type=F32)


def _in_proj_kernel(x_ref, nw_ref, w_ref, o_ref):
    xn = _rms(x_ref[...], nw_ref[...]).astype(BF16)
    o_ref[...] = jnp.dot(xn, w_ref[...], preferred_element_type=F32)


def _in_proj(x, nw, w_all, layer, n):
    m, d = x.shape
    tm = _tile(m, 512)
    return pl.pallas_call(
        _in_proj_kernel,
        grid=(m // tm,),
        in_specs=[pl.BlockSpec((tm, d), lambda i: (i, 0)),
                  pl.BlockSpec((1, d), lambda i: (0, 0)),
                  pl.BlockSpec((None, d, n), lambda i: (layer, 0, 0))],
        out_specs=pl.BlockSpec((tm, n), lambda i: (i, 0)),
        out_shape=jax.ShapeDtypeStruct((m, n), F32),
        compiler_params=pltpu.CompilerParams(dimension_semantics=("parallel",),
                                             vmem_limit_bytes=VMEM_LIMIT),
        name="in_proj",
    )(x, nw, w_all)


def _prompt_mixer_kernel(proj_ref, cos_ref, sin_ref, wm_ref, gb_ref, lnw_ref, lnb_ref,
                         dmat_ref, qdec_ref, kdec_ref, sdec_ref, gnw_ref, sinks_ref,
                         ocat_ref, sfin_ref, kout_ref, vout_ref,
                         s_scr, kprev_scr, vprev_scr):
    c = pl.program_id(1)

    @pl.when(c == 0)
    def _reset():
        s_scr[...] = jnp.zeros_like(s_scr)
        kprev_scr[...] = jnp.zeros_like(kprev_scr)
        vprev_scr[...] = jnp.zeros_like(vprev_scr)

    cos = cos_ref[...]
    sin = sin_ref[...]

    o_a, _ = _gmlp(proj_ref[:, 0:512], proj_ref[:, 512:1024], lnw_ref[...], lnb_ref[...],
                   wm_ref, gb_ref)
    ocat_ref[:, 0:512] = o_a.astype(BF16)

    q = _rope(proj_ref[:, 1024:1536], cos, sin)
    ks = _rope(proj_ref[:, 1536:2048], cos, sin) * (HEAD_DIM ** -0.5)
    v = proj_ref[:, 2048:2560]
    qd = q * qdec_ref[...]
    kd = ks * kdec_ref[...]
    ys = []
    for h in range(RET_HEADS):
        qh = _head(q, h).astype(BF16)
        kh = _head(ks, h).astype(BF16)
        vh = _head(v, h).astype(BF16)
        sc = _dot_nt(qh, kh) * dmat_ref[h]
        o = jnp.dot(sc.astype(BF16), vh, preferred_element_type=F32)
        s_old = s_scr[h]
        o = o + jnp.dot(_head(qd, h).astype(BF16), s_old.astype(BF16), preferred_element_type=F32)
        kdh = _head(kd, h).astype(BF16)
        s_new = s_old * sdec_ref[h] + lax.dot_general(
            kdh, vh, (((0,), (0,)), ((), ())), preferred_element_type=F32)
        s_scr[h] = s_new
        sfin_ref[h] = s_new
        ys.append(_group_norm(o))
    y = jnp.concatenate(ys, axis=-1) * gnw_ref[...]
    rg = proj_ref[:, 2560:3072]
    ocat_ref[:, 512:1024] = (rg * jax.nn.sigmoid(rg) * y).astype(BF16)

    qa = _rope(proj_ref[:, 3072:3584], cos, sin)
    ka = _rope(proj_ref[:, 3584:3712], cos[:, 0:128], sin[:, 0:128])
    va = proj_ref[:, 3712:3840]
    kout_ref[...] = ka
    vout_ref[...] = va
    rows = ATT_REP * CHUNK
    tt = lax.broadcasted_iota(jnp.int32, (rows, 2 * CHUNK), 0) & (CHUNK - 1)
    kk = lax.broadcasted_iota(jnp.int32, (rows, 2 * CHUNK), 1)
    valid = (kk > tt) & (kk <= tt + CHUNK) & ((kk >= CHUNK) | (c > 0))
    rr = lax.broadcasted_iota(jnp.int32, (rows, 1), 0) // CHUNK
    outs = [None] * ATT_Q_HEADS
    for g in range(ATT_KV_HEADS):
        kcat = jnp.concatenate([_head(kprev_scr[...], g), _head(ka, g)], axis=0).astype(BF16)
        vcat = jnp.concatenate([_head(vprev_scr[...], g), _head(va, g)], axis=0).astype(BF16)
        qg = jnp.concatenate([_head(qa, g * ATT_REP + r) for r in range(ATT_REP)],
                             axis=0).astype(BF16)
        s = _dot_nt(qg, kcat) * (HEAD_DIM ** -0.5)
        s = jnp.where(valid, s, NEG)
        sink = jnp.zeros((rows, 1), F32)
        for r in range(ATT_REP):
            sink = jnp.where(rr == r, sinks_ref[g * ATT_REP + r], sink)
        mx = jnp.maximum(jnp.max(s, axis=-1, keepdims=True), sink)
        e = jnp.exp(s - mx)
        den = jnp.sum(e, axis=-1, keepdims=True) + jnp.exp(sink - mx)
        p = (e / den).astype(BF16)
        o = jnp.dot(p, vcat, preferred_element_type=F32)
        for r in range(ATT_REP):
            outs[g * ATT_REP + r] = o[r * CHUNK:(r + 1) * CHUNK]
    ocat_ref[:, 1024:1536] = jnp.concatenate(outs, axis=-1).astype(BF16)
    kprev_scr[...] = ka
    vprev_scr[...] = va


def _prompt_mixer(proj, batch, seq, tabs, lw):
    nchunk = seq // CHUNK
    const2 = lambda b, c: (0, 0)
    const3 = lambda b, c: (0, 0, 0)
    row_blk = lambda b, c: (b * nchunk + c, 0)
    return pl.pallas_call(
        _prompt_mixer_kernel,
        grid=(batch, nchunk),
        in_specs=[pl.BlockSpec((CHUNK, MIX_WIDTH), row_blk),
                  pl.BlockSpec((CHUNK, 512), lambda b, c: (c, 0)),
                  pl.BlockSpec((CHUNK, 512), lambda b, c: (c, 0)),
                  pl.BlockSpec((GM_GROUPS, CHUNK, CHUNK), const3),
                  pl.BlockSpec((GM_GROUPS, CHUNK, CHUNK), const3),
                  pl.BlockSpec((1, 512), const2),
                  pl.BlockSpec((1, 512), const2),
                  pl.BlockSpec((RET_HEADS, CHUNK, CHUNK), const3),
                  pl.BlockSpec((CHUNK, 512), const2),
                  pl.BlockSpec((CHUNK, 512), const2),
                  pl.BlockSpec((RET_HEADS, HEAD_DIM, HEAD_DIM), const3),
                  pl.BlockSpec((1, 512), const2),
                  pl.BlockSpec(memory_space=pltpu.SMEM)],
        out_specs=[pl.BlockSpec((CHUNK, 3 * 512), row_blk),
                   pl.BlockSpec((None, RET_HEADS, HEAD_DIM, HEAD_DIM), lambda b, c: (b, 0, 0, 0)),
                   pl.BlockSpec((None, CHUNK, 128), lambda b, c: (b, 0, 0)),
                   pl.BlockSpec((None, CHUNK, 128), lambda b, c: (b, 0, 0))],
        out_shape=[jax.ShapeDtypeStruct((batch * seq, 3 * 512), BF16),
                   jax.ShapeDtypeStruct((batch, RET_HEADS, HEAD_DIM, HEAD_DIM), F32),
                   jax.ShapeDtypeStruct((batch, CHUNK, 128), F32),
                   jax.ShapeDtypeStruct((batch, CHUNK, 128), F32)],
        scratch_shapes=[pltpu.VMEM((RET_HEADS, HEAD_DIM, HEAD_DIM), F32),
                        pltpu.VMEM((CHUNK, 128), F32),
                        pltpu.VMEM((CHUNK, 128), F32)],
        compiler_params=pltpu.CompilerParams(dimension_semantics=("parallel", "arbitrary"),
                                             vmem_limit_bytes=VMEM_LIMIT),
        name="prompt_mixer",
    )(proj, tabs["cos_p"], tabs["sin_p"], lw["wm_p"], lw["gb_p"], lw["ln_w"], lw["ln_b"],
      tabs["dmat_p"], tabs["qdec_p"], tabs["kdec_p"], tabs["sdec_p"], lw["gn_w"], lw["sinks"])


def _sample_mixer_kernel(proj_ref, s0_ref, ck_ref, cv_ref, cos_ref, sin_ref,
                         wm_ref, gb_ref, lnw_ref, lnb_ref, dmat_ref, qdec_ref, kdec_ref,
                         sdec_ref, gnw_ref, sinks_ref,
                         ocat_ref, snew_ref, kout_ref, vout_ref, vn_ref):
    nb = s0_ref.shape[0]
    t_len = proj_ref.shape[0] // nb
    cos = cos_ref[...]
    sin = sin_ref[...]

    o_a, vn = _gmlp(proj_ref[:, 0:512], proj_ref[:, 512:1024], lnw_ref[...], lnb_ref[...],
                    wm_ref, gb_ref)
    ocat_ref[:, 0:512] = o_a.astype(BF16)
    vn_ref[...] = vn

    def seq3(x):
        return x.reshape(nb, t_len, x.shape[-1])

    q = _rope(proj_ref[:, 1024:1536], cos, sin)
    ks = _rope(proj_ref[:, 1536:2048], cos, sin) * (HEAD_DIM ** -0.5)
    v = proj_ref[:, 2048:2560]
    qd = q * qdec_ref[...]
    kd = ks * kdec_ref[...]
    ys = []
    for h in range(RET_HEADS):
        qh = _head(q, h).astype(BF16)
        kh = _head(ks, h).astype(BF16)
        vh = _head(v, h).astype(BF16)
        sc = _dot_nt(qh, kh) * dmat_ref[h]
        o = jnp.dot(sc.astype(BF16), vh, preferred_element_type=F32)
        s_old = s0_ref[:, h]
        oc = jnp.einsum("ntd,nde->nte", seq3(_head(qd, h)).astype(BF16), s_old.astype(BF16),
                        preferred_element_type=F32)
        o = o + oc.reshape(nb * t_len, HEAD_DIM)
        kd3t = jnp.swapaxes(seq3(_head(kd, h)), 1, 2).astype(BF16)
        upd = jnp.einsum("ndt,nte->nde", kd3t, seq3(_head(v, h)).astype(BF16),
                         preferred_element_type=F32)
        snew_ref[:, h] = s_old * sdec_ref[h] + upd
        ys.append(_group_norm(o))
    y = jnp.concatenate(ys, axis=-1) * gnw_ref[...]
    rg = proj_ref[:, 2560:3072]
    ocat_ref[:, 512:1024] = (rg * jax.nn.sigmoid(rg) * y).astype(BF16)

    qa = _rope(proj_ref[:, 3072:3584], cos, sin)
    ka = _rope(proj_ref[:, 3584:3712], cos[:, 0:128], sin[:, 0:128])
    va = proj_ref[:, 3712:3840]
    wb = ck_ref.shape[1]
    kout_ref[:, 0:wb - t_len, :] = ck_ref[:, t_len:wb, :]
    kout_ref[:, wb - t_len:wb, :] = seq3(ka)
    vout_ref[:, 0:wb - t_len, :] = cv_ref[:, t_len:wb, :]
    vout_ref[:, wb - t_len:wb, :] = seq3(va)
    nq = ATT_REP * t_len
    tq_c = lax.broadcasted_iota(jnp.int32, (nb, nq, wb), 1) & (t_len - 1)
    kk_c = lax.broadcasted_iota(jnp.int32, (nb, nq, wb), 2)
    valid_c = kk_c > tq_c
    tq_n = lax.broadcasted_iota(jnp.int32, (nb, nq, t_len), 1) & (t_len - 1)
    kk_n = lax.broadcasted_iota(jnp.int32, (nb, nq, t_len), 2)
    valid_n = kk_n <= tq_n
    rr = lax.broadcasted_iota(jnp.int32, (1, nq, 1), 1) // t_len
    outs = [None] * ATT_Q_HEADS
    for g in range(ATT_KV_HEADS):
        gs = slice(g * HEAD_DIM, (g + 1) * HEAD_DIM)
        kc = ck_ref[:, :, gs].astype(BF16)
        vc = cv_ref[:, :, gs].astype(BF16)
        kn = seq3(_head(ka, g)).astype(BF16)
        vnw = seq3(_head(va, g)).astype(BF16)
        qg = jnp.concatenate([seq3(_head(qa, g * ATT_REP + r)) for r in range(ATT_REP)],
                             axis=1).astype(BF16)
        s_c = jnp.einsum("nqd,nkd->nqk", qg, kc, preferred_element_type=F32) * (HEAD_DIM ** -0.5)
        s_n = jnp.einsum("nqd,nkd->nqk", qg, kn, preferred_element_type=F32) * (HEAD_DIM ** -0.5)
        s_c = jnp.where(valid_c, s_c, NEG)
        s_n = jnp.where(valid_n, s_n, NEG)
        sink = jnp.zeros((1, nq, 1), F32)
        for r in range(ATT_REP):
            sink = jnp.where(rr == r, sinks_ref[g * ATT_REP + r], sink)
        mx = jnp.maximum(jnp.maximum(jnp.max(s_c, axis=-1, keepdims=True),
                                     jnp.max(s_n, axis=-1, keepdims=True)), sink)
        e_c = jnp.exp(s_c - mx)
        e_n = jnp.exp(s_n - mx)
        den = (jnp.sum(e_c, axis=-1, keepdims=True) + jnp.sum(e_n, axis=-1, keepdims=True)
               + jnp.exp(sink - mx))
        o = (jnp.einsum("nqk,nkd->nqd", (e_c / den).astype(BF16), vc, preferred_element_type=F32)
             + jnp.einsum("nqk,nkd->nqd", (e_n / den).astype(BF16), vnw,
                          preferred_element_type=F32))
        for r in range(ATT_REP):
            outs[g * ATT_REP + r] = o[:, r * t_len:(r + 1) * t_len, :].reshape(nb * t_len, HEAD_DIM)
    ocat_ref[:, 1024:1536] = jnp.concatenate(outs, axis=-1).astype(BF16)


def _sample_mixer(proj, s0, ck, cv, tabs, lw):
    ns, wb = ck.shape[0], ck.shape[1]
    nb = SEQ_PER_STEP
    t_len = CHUNK // nb
    const2 = lambda i: (0, 0)
    const3 = lambda i: (0, 0, 0)
    row_blk = lambda i: (i, 0)
    seq_blk3 = lambda i: (i, 0, 0)
    return pl.pallas_call(
        _sample_mixer_kernel,
        grid=(ns // nb,),
        in_specs=[pl.BlockSpec((CHUNK, MIX_WIDTH), row_blk),
                  pl.BlockSpec((nb, RET_HEADS, HEAD_DIM, HEAD_DIM), lambda i: (i, 0, 0, 0)),
                  pl.BlockSpec((nb, wb, 128), seq_blk3),
                  pl.BlockSpec((nb, wb, 128), seq_blk3),
                  pl.BlockSpec((CHUNK, 512), const2),
                  pl.BlockSpec((CHUNK, 512), const2),
                  pl.BlockSpec((GM_GROUPS, CHUNK, CHUNK), const3),
                  pl.BlockSpec((GM_GROUPS, CHUNK, CHUNK), const3),
                  pl.BlockSpec((1, 512), const2),
                  pl.BlockSpec((1, 512), const2),
                  pl.BlockSpec((RET_HEADS, CHUNK, CHUNK), const3),
                  pl.BlockSpec((CHUNK, 512), const2),
                  pl.BlockSpec((CHUNK, 512), const2),
                  pl.BlockSpec((RET_HEADS, HEAD_DIM, HEAD_DIM), const3),
                  pl.BlockSpec((1, 512), const2),
                  pl.BlockSpec(memory_space=pltpu.SMEM)],
        out_specs=[pl.BlockSpec((CHUNK, 3 * 512), row_blk),
                   pl.BlockSpec((nb, RET_HEADS, HEAD_DIM, HEAD_DIM), lambda i: (i, 0, 0, 0)),
                   pl.BlockSpec((nb, wb, 128), seq_blk3),
                   pl.BlockSpec((nb, wb, 128), seq_blk3),
                   pl.BlockSpec((CHUNK, GM_WIDTH), lambda i: (i, 0))],
        out_shape=[jax.ShapeDtypeStruct((ns * t_len, 3 * 512), BF16),
                   jax.ShapeDtypeStruct(s0.shape, F32),
                   jax.ShapeDtypeStruct(ck.shape, F32),
                   jax.ShapeDtypeStruct(cv.shape, F32),
                   jax.ShapeDtypeStruct((ns * t_len, GM_WIDTH), F32)],
        compiler_params=pltpu.CompilerParams(dimension_semantics=("parallel",),
                                             vmem_limit_bytes=VMEM_LIMIT),
        name="sample_mixer",
    )(proj, s0, ck, cv, tabs["cos_s"], tabs["sin_s"], lw["wm_s"], lw["gb_s"],
      lw["ln_w"], lw["ln_b"], tabs["dmat_s"], tabs["qdec_s"], tabs["kdec_s"], tabs["sdec_s"],
      lw["gn_w"], lw["sinks"])


def _gate_out_kernel(x_ref, oc_ref, nw_ref, wg_ref, bg_ref, wbr_ref, wout_ref, h_ref):
    x = x_ref[...]
    xn = _rms(x, nw_ref[...]).astype(BF16)
    m = None
    for b in range(3):
        cols = slice(b * D_MODEL, (b + 1) * D_MODEL)
        gate = jax.nn.sigmoid(jnp.dot(xn, wg_ref[:, cols], preferred_element_type=F32)
                              + bg_ref[:, cols])
        p = jnp.dot(oc_ref[:, b * 512:(b + 1) * 512], wbr_ref[b * 512:(b + 1) * 512, :],
                    preferred_element_type=F32)
        m = gate * p if m is None else m + gate * p
    h_ref[...] = x + jnp.dot(m.astype(BF16), wout_ref[...], preferred_element_type=F32)


def _gate_out(x, ocat, nw, wg, bg, wbr, wout):
    m, d = x.shape
    tm = _tile(m, 512)
    const = lambda i: (0, 0)
    return pl.pallas_call(
        _gate_out_kernel,
        grid=(m // tm,),
        in_specs=[pl.BlockSpec((tm, d), lambda i: (i, 0)),
                  pl.BlockSpec((tm, 3 * 512), lambda i: (i, 0)),
                  pl.BlockSpec((1, d), const),
                  pl.BlockSpec((d, GATE_WIDTH), const),
                  pl.BlockSpec((1, GATE_WIDTH), const),
                  pl.BlockSpec((3 * 512, d), const),
                  pl.BlockSpec((d, d), const)],
        out_specs=pl.BlockSpec((tm, d), lambda i: (i, 0)),
        out_shape=jax.ShapeDtypeStruct((m, d), F32),
        compiler_params=pltpu.CompilerParams(dimension_semantics=("parallel",),
                                             vmem_limit_bytes=VMEM_LIMIT),
        name="gate_out",
    )(x, ocat, nw, wg, bg, wbr, wout)


def _col_max(w):
    return jnp.max(w, axis=0, keepdims=True)


def _top_sorted(w, with_rank):
    row = lax.broadcasted_iota(jnp.int32, (PEER_TOPK, w.shape[1]), 0)
    out = jnp.zeros((PEER_TOPK, w.shape[1]), F32)
    rank = jnp.full(w.shape, float(PEER_TOPK), F32)
    for a in range(PEER_TOPK):
        m = _col_max(w)
        out = jnp.where(row == a, m, out)
        top = w == m
        if with_rank:
            rank = jnp.where(top, float(a), rank)
        if a + 1 < PEER_TOPK:
            w = jnp.where(top, -jnp.inf, w)
    return out, rank


def _select_counts(v1, v2):
    blocks = [v2 + v1[0:1]]
    for a in range(1, 8):
        blocks.append(v2[0:8] + v1[a:a + 1])
    blocks.append(v1[8:16] + v2[0:1])
    cand = jnp.concatenate(blocks, axis=0)
    cell = lax.broadcasted_iota(jnp.int32, cand.shape, 0)
    w = cand
    ones = jnp.zeros_like(cand)
    for _ in range(PEER_TOPK):
        first = jnp.min(jnp.where(w == _col_max(w), cell, cand.shape[0]), axis=0, keepdims=True)
        pick = cell == first
        ones = jnp.where(pick, 1.0, ones)
        w = jnp.where(pick, -jnp.inf, w)
    e = jnp.where(ones > 0.0, jnp.exp(cand - (v1[0:1] + v2[0:1])), 0.0)
    row = lax.broadcasted_iota(jnp.int32, (8, v1.shape[1]), 0)
    low = jnp.zeros((8, v1.shape[1]), F32)
    for a in range(8):
        lo = 0 if a == 0 else 8 + 8 * a
        hi = 16 if a == 0 else lo + 8
        low = jnp.where(row == a, jnp.sum(ones[lo:hi], axis=0, keepdims=True), low)
    counts = jnp.concatenate([low, ones[72:80]], axis=0)
    return counts, 1.0 / jnp.sum(e, axis=0, keepdims=True)


def _peer_kernel(*refs, final_norm):
    if final_norm:
        (h_ref, nw_ref, wqt_ref, k1_ref, k2_ref, pu_ref, pvt_ref, nf_ref, out_ref,
         xnt_scr, cnt_scr, a_scr, rank_scr, b_scr, z_scr, acc_scr) = refs
    else:
        (h_ref, nw_ref, wqt_ref, k1_ref, k2_ref, pu_ref, pvt_ref, out_ref,
         xnt_scr, cnt_scr, a_scr, rank_scr, b_scr, z_scr, acc_scr) = refs
        nf_ref = None
    j = pl.program_id(1)
    tm = h_ref.shape[0]
    nlc = tm // LANES
    nsub = pu_ref.shape[0] // PEER_NKEYS
    dq = 2 * PEER_NKEYS
    quarter = PEER_QUARTER_KEYS * PEER_NKEYS

    @pl.when(j == 0)
    def _prepare():
        xn = _rms(h_ref[...], nw_ref[...])
        xnt = xn.T.astype(BF16)
        xnt_scr[...] = xnt
        qt = jnp.dot(wqt_ref[...], xnt, preferred_element_type=F32).astype(BF16)
        for h in range(PEER_HEADS):
            s1 = jnp.dot(k1_ref[...], qt[h * dq:h * dq + PEER_NKEYS], preferred_element_type=F32)
            s2 = jnp.dot(k2_ref[...], qt[h * dq + PEER_NKEYS:(h + 1) * dq],
                         preferred_element_type=F32)
            for lc in range(nlc):
                cs = slice(lc * LANES, (lc + 1) * LANES)
                s1c = s1[:, cs]
                s2c = s2[:, cs]
                v1, _ = _top_sorted(s1c, False)
                v2, rank2 = _top_sorted(s2c, True)
                counts, zinv = _select_counts(v1, v2)
                cnt = jnp.zeros_like(s1c)
                for a in range(PEER_TOPK):
                    cnt = jnp.where(s1c == v1[a:a + 1], counts[a:a + 1], cnt)
                cnt_scr[lc, pl.ds(h, PEER_NKEYS, stride=PEER_HEADS), :] = cnt
                a_scr[lc, pl.ds(h, PEER_NKEYS, stride=PEER_HEADS), :] = jnp.exp(s1c - v1[0:1])
                rank_scr[h, :, cs] = rank2
                b_scr[h, :, cs] = jnp.exp(s2c - v2[0:1]) * zinv
        acc_scr[...] = jnp.zeros_like(acc_scr)

    base = pl.multiple_of(j * nsub, nsub)

    def activations(q):
        return jnp.dot(pu_ref[q * quarter:(q + 1) * quarter, :], xnt_scr[...],
                       preferred_element_type=F32)

    def weigh(act, q):
        for lc in range(nlc):
            cs = slice(lc * LANES, (lc + 1) * LANES)
            keys = [q * PEER_QUARTER_KEYS + ii for ii in range(PEER_QUARTER_KEYS)]
            heads = [pl.ds(pl.multiple_of((base + r) * PEER_HEADS, PEER_HEADS), PEER_HEADS) for r in keys]
            crows = [cnt_scr[lc, hs, :] for hs in heads]
            arows = [a_scr[lc, hs, :] for hs in heads]
            ws = [jnp.zeros((PEER_NKEYS, LANES), F32) for _ in keys]
            for h in range(PEER_HEADS):
                rank = rank_scr[h, :, cs]
                b = b_scr[h, :, cs]
                for n, r in enumerate(keys):
                    hit = rank < crows[n][h:h + 1]
                    ws[n] = ws[n] + jnp.where(hit, b, 0.0) * arows[n][h:h + 1]
            for n, r in enumerate(keys):
                g = _gelu(act[n * PEER_NKEYS:(n + 1) * PEER_NKEYS, cs])
                z_scr[r * PEER_NKEYS:(r + 1) * PEER_NKEYS, cs] = (ws[n] * g).astype(BF16)

    for q in range(nsub // PEER_QUARTER_KEYS):
        weigh(activations(q), q)
    acc_scr[...] += jnp.dot(pvt_ref[...], z_scr[...], preferred_element_type=F32)

    @pl.when(j == pl.num_programs(1) - 1)
    def _finish():
        y = h_ref[...] + acc_scr[...].T
        if final_norm:
            y = _rms(y, nf_ref[...])
        out_ref[...] = y


def _peer(h, nw, wqt, k1, k2, pu, pvt, nf, layer):
    m, d = h.shape
    ne = pu.shape[1]
    tm = _tile(m, 512)
    eb = PEER_EXPERT_BLOCK
    assert eb == 8 * PEER_NKEYS and ne % eb == 0
    const = lambda i, j: (0, 0)
    in_specs = [pl.BlockSpec((tm, d), lambda i, j: (i, 0)),
                pl.BlockSpec((1, d), const),
                pl.BlockSpec((None,) + wqt.shape[1:], lambda i, j: (layer, 0, 0)),
                pl.BlockSpec(k1.shape, const),
                pl.BlockSpec(k2.shape, const),
                pl.BlockSpec((None, eb, d), lambda i, j: (layer, j, 0)),
                pl.BlockSpec((None, d, eb), lambda i, j: (layer, 0, j))]
    args = [h, nw, wqt, k1, k2, pu, pvt]
    if nf is not None:
        in_specs.append(pl.BlockSpec((1, d), const))
        args.append(nf)
    sel = pltpu.VMEM((PEER_HEADS, PEER_NKEYS, tm), F32)
    sel_kh = pltpu.VMEM((tm // LANES, PEER_NKEYS * PEER_HEADS, LANES), F32)
    return pl.pallas_call(
        functools.partial(_peer_kernel, final_norm=nf is not None),
        grid=(m // tm, ne // eb),
        in_specs=in_specs,
        out_specs=pl.BlockSpec((tm, d), lambda i, j: (i, 0)),
        out_shape=jax.ShapeDtypeStruct((m, d), F32),
        scratch_shapes=[pltpu.VMEM((d, tm), BF16), sel_kh, sel_kh, sel, sel,
                        pltpu.VMEM((eb, tm), BF16),
                        pltpu.VMEM((d, tm), F32)],
        compiler_params=pltpu.CompilerParams(dimension_semantics=("parallel", "arbitrary"),
                                             vmem_limit_bytes=VMEM_LIMIT),
        name="peer",
    )(*args)


def _rope_tables(pos):
    half = HEAD_DIM // 2
    inv = jnp.exp(-math.log(ROPE_THETA) * jnp.arange(half, dtype=F32) / half)
    ang = pos.astype(F32)[:, None] * inv[None, :]
    cos, sin = jnp.cos(ang), jnp.sin(ang)
    return (jnp.tile(jnp.concatenate([cos, cos], axis=-1), (1, RET_HEADS)),
            jnp.tile(jnp.concatenate([-sin, sin], axis=-1), (1, RET_HEADS)))


def _decay_tables(t_idx, same_seq, c_len):
    lg = jnp.log1p(-jnp.exp2(-5.0 - jnp.arange(RET_HEADS, dtype=F32)))[:, None]
    t = t_idx.astype(F32)
    diff = t[:, None] - t[None, :]
    dmat = jnp.where((diff[None] >= 0) & same_seq[None],
                     jnp.exp(jnp.maximum(diff, 0.0)[None] * lg[:, :, None]), 0.0)
    qdec = jnp.repeat(jnp.exp((t + 1.0)[None] * lg).T, HEAD_DIM, axis=1)
    kdec = jnp.repeat(jnp.exp((c_len - 1.0 - t)[None] * lg).T, HEAD_DIM, axis=1)
    sdec = jnp.broadcast_to(jnp.exp(c_len * lg)[:, :, None], (RET_HEADS, HEAD_DIM, HEAD_DIM))
    return dmat, qdec, kdec, sdec


def kernel(x_prompt, x_sample, state_ret, cache_win_k, cache_win_v, norm1_w, norm2_w, normf_w, w_in, b_gate, gm_ln_w, gm_ln_b, gm_ws, gm_b, ret_gn_w, attn_sinks, w_br_a, w_br_b, w_br_c, w_out, peer_wq, peer_k1, peer_k2, peer_u, peer_v):
    batch, seq, d = x_prompt.shape
    ns, t_len, _ = x_sample.shape
    depth = w_in.shape[0]
    wb = cache_win_k.shape[2]
    assert d == D_MODEL and seq % CHUNK == 0 and t_len * SEQ_PER_STEP == CHUNK
    assert ns % SEQ_PER_STEP == 0 and wb == CHUNK
    xp = x_prompt.reshape(batch * seq, d)
    xs = x_sample.reshape(ns * t_len, d)

    row = jnp.arange(CHUNK)
    tabs = {}
    tabs["cos_p"], tabs["sin_p"] = _rope_tables(jnp.arange(seq))
    tabs["cos_s"], tabs["sin_s"] = _rope_tables(PAST_LEN + row % t_len)
    (tabs["dmat_p"], tabs["qdec_p"], tabs["kdec_p"], tabs["sdec_p"]) = _decay_tables(
        row, jnp.ones((CHUNK, CHUNK), bool), float(CHUNK))
    (tabs["dmat_s"], tabs["qdec_s"], tabs["kdec_s"], tabs["sdec_s"]) = _decay_tables(
        row % t_len, (row // t_len)[:, None] == (row // t_len)[None, :], float(t_len))
    causal = jnp.tril(jnp.ones((CHUNK, CHUNK), bool))
    eye = jnp.eye(SEQ_PER_STEP, dtype=F32)

    w_in_all = w_in.astype(BF16)
    wqt_all = jnp.swapaxes(peer_wq, 1, 2).astype(BF16)
    pu_all = peer_u.astype(BF16)
    pvt_all = jnp.swapaxes(peer_v, 1, 2).astype(BF16)

    s_p, s_s, k_p, v_p, k_s, v_s, g_s = [], [], [], [], [], [], []
    for l in range(depth):
        ws = gm_ws[l]
        causal_s = jnp.tril(jnp.ones((t_len, t_len), bool))
        ws_s = jnp.where(causal_s[None], ws[:, :t_len, :t_len], 0.0)
        lw = {
            "wm_p": jnp.where(causal[None], ws, 0.0).astype(BF16),
            "gb_p": jnp.broadcast_to(gm_b[l][:, :, None], (GM_GROUPS, CHUNK, CHUNK)),
            "wm_s": jnp.stack([jnp.kron(eye, ws_s[g]) for g in range(GM_GROUPS)]).astype(BF16),
            "gb_s": jnp.broadcast_to(jnp.tile(gm_b[l][:, :t_len], (1, SEQ_PER_STEP))[:, :, None],
                                     (GM_GROUPS, CHUNK, CHUNK)),
            "ln_w": gm_ln_w[l][None], "ln_b": gm_ln_b[l][None],
            "gn_w": ret_gn_w[l][None], "sinks": attn_sinks[l],
        }
        n1 = norm1_w[l][None]
        proj_p = _in_proj(xp, n1, w_in_all, l, MIX_WIDTH)
        proj_s = _in_proj(xs, n1, w_in_all, l, MIX_WIDTH)
        ocat_p, sp, kp, vp = _prompt_mixer(proj_p, batch, seq, tabs, lw)
        ocat_s, ss, ks, vs, gs = _sample_mixer(
            proj_s, state_ret[l], cache_win_k[l].reshape(ns, wb, 128),
            cache_win_v[l].reshape(ns, wb, 128), tabs, lw)
        gate_w = (n1, w_in_all[l, :, MIX_WIDTH:], b_gate[l][None],
                  jnp.concatenate([w_br_a[l], w_br_b[l], w_br_c[l]], axis=0).astype(BF16),
                  w_out[l].astype(BF16))
        peer_w = (norm2_w[l][None], wqt_all, peer_k1[l].astype(BF16), peer_k2[l].astype(BF16),
                  pu_all, pvt_all, normf_w[None] if l == depth - 1 else None, l)
        xp = _peer(_gate_out(xp, ocat_p, *gate_w), *peer_w)
        xs = _peer(_gate_out(xs, ocat_s, *gate_w), *peer_w)
        s_p.append(sp)
        s_s.append(ss)
        k_p.append(kp.reshape(batch, CHUNK, ATT_KV_HEADS, HEAD_DIM))
        v_p.append(vp.reshape(batch, CHUNK, ATT_KV_HEADS, HEAD_DIM))
        k_s.append(ks.reshape(ns, wb, ATT_KV_HEADS, HEAD_DIM))
        v_s.append(vs.reshape(ns, wb, ATT_KV_HEADS, HEAD_DIM))
        g_s.append(gs.reshape(ns, t_len, GM_WIDTH))
    return (xp.reshape(batch, seq, d), xs.reshape(ns, t_len, d),
            jnp.stack(s_p), jnp.stack(s_s), jnp.stack(k_p), jnp.stack(v_p),
            jnp.stack(k_s), jnp.stack(v_s), jnp.stack(g_s))
```

```python
import functools
import math

import numpy as np
import jax
import jax.numpy as jnp
from jax import lax
from jax.experimental import pallas as pl
from jax.experimental.pallas import tpu as pltpu

F32 = jnp.float32
BF16 = jnp.bfloat16

D_MODEL = 1024
HEAD_DIM = 64
CHUNK = 128
GM_WIDTH = 512
GM_GROUPS = 4
RET_HEADS = 8
ATT_Q_HEADS = 8
ATT_KV_HEADS = 2
ATT_REP = 4
ROPE_THETA = 10000.0
PAST_LEN = 16384
PEER_HEADS = 8
PEER_NKEYS = 128
PEER_TOPK = 16
EPS = 1e-6
NEG = -1e30
MIX_WIDTH = 3840
GATE_WIDTH = 3 * D_MODEL
SEQ_PER_STEP = 16
LANES = 128
PEER_EXPERT_BLOCK = 1024
PEER_QUARTER_KEYS = 2
VMEM_LIMIT = 56 * 1024 * 1024


def _tile(m, target):
    t = min(m, target)
    while m % t or t % LANES:
        t -= LANES
    return t


def _rms(x, w):
    return x * lax.rsqrt(jnp.mean(x * x, axis=-1, keepdims=True) + EPS) * w


def _gelu(x):
    return 0.5 * x * (1.0 + lax.erf(x * np.float32(math.sqrt(0.5))))


def _rope(x, cos, sin_signed):
    w = x.shape[-1]
    lane = lax.broadcasted_iota(jnp.int32, x.shape, 1)
    first = (lane & (HEAD_DIM - 1)) < HEAD_DIM // 2
    rot = jnp.where(first, pltpu.roll(x, w - HEAD_DIM // 2, 1), pltpu.roll(x, HEAD_DIM // 2, 1))
    return x * cos + rot * sin_signed


def _head(x, h):
    return x[:, h * HEAD_DIM:(h + 1) * HEAD_DIM]


def _gmlp(gu, gv, lnw, lnb, wm_ref, gb_ref):
    u = _gelu(gu)
    vf = _gelu(gv)
    mu = jnp.mean(vf, axis=-1, keepdims=True)
    var = jnp.mean(jnp.square(vf - mu), axis=-1, keepdims=True)
    vn = (vf - mu) * lax.rsqrt(var + EPS) * lnw + lnb
    vnb = vn.astype(BF16)
    parts = []
    for g in range(GM_GROUPS):
        s = jnp.dot(wm_ref[g], vnb[:, g * LANES:(g + 1) * LANES], preferred_element_type=F32)
        parts.append(s + gb_ref[g])
    return u * jnp.concatenate(parts, axis=-1), vn


def _group_norm(o):
    mu = jnp.mean(o, axis=-1, keepdims=True)
    var = jnp.mean(jnp.square(o - mu), axis=-1, keepdims=True)
    return (o - mu) * lax.rsqrt(var + EPS)


def _dot_nt(a, b):
    return lax.dot_general(a, b, (((1,), (1,)), ((), ())), preferred_element_type=F32)


def _in_proj_kernel(x_ref, nw_ref, w_ref, o_ref):
    xn = _rms(x_ref[...], nw_ref[...]).astype(BF16)
    o_ref[...] = jnp.dot(xn, w_ref[...], preferred_element_type=F32)


def _in_proj(x, nw, w_all, layer, n):
    m, d = x.shape
    tm = _tile(m, 512)
    return pl.pallas_call(
        _in_proj_kernel,
        grid=(m // tm,),
        in_specs=[pl.BlockSpec((tm, d), lambda i: (i, 0)),
                  pl.BlockSpec((1, d), lambda i: (0, 0)),
                  pl.BlockSpec((None, d, n), lambda i: (layer, 0, 0))],
        out_specs=pl.BlockSpec((tm, n), lambda i: (i, 0)),
        out_shape=jax.ShapeDtypeStruct((m, n), F32),
        compiler_params=pltpu.CompilerParams(dimension_semantics=("parallel",),
                                             vmem_limit_bytes=VMEM_LIMIT),
        name="in_proj",
    )(x, nw, w_all)


def _prompt_mixer_kernel(proj_ref, cos_ref, sin_ref, wm_ref, gb_ref, lnw_ref, lnb_ref,
                         dmat_ref, qdec_ref, kdec_ref, sdec_ref, gnw_ref, sinks_ref,
                         ocat_ref, sfin_ref, kout_ref, vout_ref,
                         s_scr, kprev_scr, vprev_scr):
    c = pl.program_id(1)

    @pl.when(c == 0)
    def _reset():
        s_scr[...] = jnp.zeros_like(s_scr)
        kprev_scr[...] = jnp.zeros_like(kprev_scr)
        vprev_scr[...] = jnp.zeros_like(vprev_scr)

    cos = cos_ref[...]
    sin = sin_ref[...]

    o_a, _ = _gmlp(proj_ref[:, 0:512], proj_ref[:, 512:1024], lnw_ref[...], lnb_ref[...],
                   wm_ref, gb_ref)
    ocat_ref[:, 0:512] = o_a.astype(BF16)

    q = _rope(proj_ref[:, 1024:1536], cos, sin)
    ks = _rope(proj_ref[:, 1536:2048], cos, sin) * (HEAD_DIM ** -0.5)
    v = proj_ref[:, 2048:2560]
    qd = q * qdec_ref[...]
    kd = ks * kdec_ref[...]
    ys = []
    for h in range(RET_HEADS):
        qh = _head(q, h).astype(BF16)
        kh = _head(ks, h).astype(BF16)
        vh = _head(v, h).astype(BF16)
        sc = _dot_nt(qh, kh) * dmat_ref[h]
        o = jnp.dot(sc.astype(BF16), vh, preferred_element_type=F32)
        s_old = s_scr[h]
        o = o + jnp.dot(_head(qd, h).astype(BF16), s_old.astype(BF16), preferred_element_type=F32)
        kdh = _head(kd, h).astype(BF16)
        s_new = s_old * sdec_ref[h] + lax.dot_general(
            kdh, vh, (((0,), (0,)), ((), ())), preferred_element_type=F32)
        s_scr[h] = s_new
        sfin_ref[h] = s_new
        ys.append(_group_norm(o))
    y = jnp.concatenate(ys, axis=-1) * gnw_ref[...]
    rg = proj_ref[:, 2560:3072]
    ocat_ref[:, 512:1024] = (rg * jax.nn.sigmoid(rg) * y).astype(BF16)

    qa = _rope(proj_ref[:, 3072:3584], cos, sin)
    ka = _rope(proj_ref[:, 3584:3712], cos[:, 0:128], sin[:, 0:128])
    va = proj_ref[:, 3712:3840]
    kout_ref[...] = ka
    vout_ref[...] = va
    rows = ATT_REP * CHUNK
    tt = lax.broadcasted_iota(jnp.int32, (rows, 2 * CHUNK), 0) & (CHUNK - 1)
    kk = lax.broadcasted_iota(jnp.int32, (rows, 2 * CHUNK), 1)
    valid = (kk > tt) & (kk <= tt + CHUNK) & ((kk >= CHUNK) | (c > 0))
    rr = lax.broadcasted_iota(jnp.int32, (rows, 1), 0) // CHUNK
    outs = [None] * ATT_Q_HEADS
    for g in range(ATT_KV_HEADS):
        kcat = jnp.concatenate([_head(kprev_scr[...], g), _head(ka, g)], axis=0).astype(BF16)
        vcat = jnp.concatenate([_head(vprev_scr[...], g), _head(va, g)], axis=0).astype(BF16)
        qg = jnp.concatenate([_head(qa, g * ATT_REP + r) for r in range(ATT_REP)],
                             axis=0).astype(BF16)
        s = _dot_nt(qg, kcat) * (HEAD_DIM ** -0.5)
        s = jnp.where(valid, s, NEG)
        sink = jnp.zeros((rows, 1), F32)
        for r in range(ATT_REP):
            sink = jnp.where(rr == r, sinks_ref[g * ATT_REP + r], sink)
        mx = jnp.maximum(jnp.max(s, axis=-1, keepdims=True), sink)
        e = jnp.exp(s - mx)
        den = jnp.sum(e, axis=-1, keepdims=True) + jnp.exp(sink - mx)
        p = (e / den).astype(BF16)
        o = jnp.dot(p, vcat, preferred_element_type=F32)
        for r in range(ATT_REP):
            outs[g * ATT_REP + r] = o[r * CHUNK:(r + 1) * CHUNK]
    ocat_ref[:, 1024:1536] = jnp.concatenate(outs, axis=-1).astype(BF16)
    kprev_scr[...] = ka
    vprev_scr[...] = va


def _prompt_mixer(proj, batch, seq, tabs, lw):
    nchunk = seq // CHUNK
    const2 = lambda b, c: (0, 0)
    const3 = lambda b, c: (0, 0, 0)
    row_blk = lambda b, c: (b * nchunk + c, 0)
    return pl.pallas_call(
        _prompt_mixer_kernel,
        grid=(batch, nchunk),
        in_specs=[pl.BlockSpec((CHUNK, MIX_WIDTH), row_blk),
                  pl.BlockSpec((CHUNK, 512), lambda b, c: (c, 0)),
                  pl.BlockSpec((CHUNK, 512), lambda b, c: (c, 0)),
                  pl.BlockSpec((GM_GROUPS, CHUNK, CHUNK), const3),
                  pl.BlockSpec((GM_GROUPS, CHUNK, CHUNK), const3),
                  pl.BlockSpec((1, 512), const2),
                  pl.BlockSpec((1, 512), const2),
                  pl.BlockSpec((RET_HEADS, CHUNK, CHUNK), const3),
                  pl.BlockSpec((CHUNK, 512), const2),
                  pl.BlockSpec((CHUNK, 512), const2),
                  pl.BlockSpec((RET_HEADS, HEAD_DIM, HEAD_DIM), const3),
                  pl.BlockSpec((1, 512), const2),
                  pl.BlockSpec(memory_space=pltpu.SMEM)],
        out_specs=[pl.BlockSpec((CHUNK, 3 * 512), row_blk),
                   pl.BlockSpec((None, RET_HEADS, HEAD_DIM, HEAD_DIM), lambda b, c: (b, 0, 0, 0)),
                   pl.BlockSpec((None, CHUNK, 128), lambda b, c: (b, 0, 0)),
                   pl.BlockSpec((None, CHUNK, 128), lambda b, c: (b, 0, 0))],
        out_shape=[jax.ShapeDtypeStruct((batch * seq, 3 * 512), BF16),
                   jax.ShapeDtypeStruct((batch, RET_HEADS, HEAD_DIM, HEAD_DIM), F32),
                   jax.ShapeDtypeStruct((batch, CHUNK, 128), F32),
                   jax.ShapeDtypeStruct((batch, CHUNK, 128), F32)],
        scratch_shapes=[pltpu.VMEM((RET_HEADS, HEAD_DIM, HEAD_DIM), F32),
                        pltpu.VMEM((CHUNK, 128), F32),
                        pltpu.VMEM((CHUNK, 128), F32)],
        compiler_params=pltpu.CompilerParams(dimension_semantics=("parallel", "arbitrary"),
                                             vmem_limit_bytes=VMEM_LIMIT),
        name="prompt_mixer",
    )(proj, tabs["cos_p"], tabs["sin_p"], lw["wm_p"], lw["gb_p"], lw["ln_w"], lw["ln_b"],
      tabs["dmat_p"], tabs["qdec_p"], tabs["kdec_p"], tabs["sdec_p"], lw["gn_w"], lw["sinks"])


def _sample_mixer_kernel(proj_ref, s0_ref, ck_ref, cv_ref, cos_ref, sin_ref,
                         wm_ref, gb_ref, lnw_ref, lnb_ref, dmat_ref, qdec_ref, kdec_ref,
                         sdec_ref, gnw_ref, sinks_ref,
                         ocat_ref, snew_ref, kout_ref, vout_ref, vn_ref):
    nb = s0_ref.shape[0]
    t_len = proj_ref.shape[0] // nb
    cos = cos_ref[...]
    sin = sin_ref[...]

    o_a, vn = _gmlp(proj_ref[:, 0:512], proj_ref[:, 512:1024], lnw_ref[...], lnb_ref[...],
                    wm_ref, gb_ref)
    ocat_ref[:, 0:512] = o_a.astype(BF16)
    vn_ref[...] = vn

    def seq3(x):
        return x.reshape(nb, t_len, x.shape[-1])

    q = _rope(proj_ref[:, 1024:1536], cos, sin)
    ks = _rope(proj_ref[:, 1536:2048], cos, sin) * (HEAD_DIM ** -0.5)
    v = proj_ref[:, 2048:2560]
    qd = q * qdec_ref[...]
    kd = ks * kdec_ref[...]
    ys = []
    for h in range(RET_HEADS):
        qh = _head(q, h).astype(BF16)
        kh = _head(ks, h).astype(BF16)
        vh = _head(v, h).astype(BF16)
        sc = _dot_nt(qh, kh) * dmat_ref[h]
        o = jnp.dot(sc.astype(BF16), vh, preferred_element_type=F32)
        s_old = s0_ref[:, h]
        oc = jnp.einsum("ntd,nde->nte", seq3(_head(qd, h)).astype(BF16), s_old.astype(BF16),
                        preferred_element_type=F32)
        o = o + oc.reshape(nb * t_len, HEAD_DIM)
        kd3t = jnp.swapaxes(seq3(_head(kd, h)), 1, 2).astype(BF16)
        upd = jnp.einsum("ndt,nte->nde", kd3t, seq3(_head(v, h)).astype(BF16),
                         preferred_element_type=F32)
        snew_ref[:, h] = s_old * sdec_ref[h] + upd
        ys.append(_group_norm(o))
    y = jnp.concatenate(ys, axis=-1) * gnw_ref[...]
    rg = proj_ref[:, 2560:3072]
    ocat_ref[:, 512:1024] = (rg * jax.nn.sigmoid(rg) * y).astype(BF16)

    qa = _rope(proj_ref[:, 3072:3584], cos, sin)
    ka = _rope(proj_ref[:, 3584:3712], cos[:, 0:128], sin[:, 0:128])
    va = proj_ref[:, 3712:3840]
    wb = ck_ref.shape[1]
    kout_ref[:, 0:wb - t_len, :] = ck_ref[:, t_len:wb, :]
    kout_ref[:, wb - t_len:wb, :] = seq3(ka)
    vout_ref[:, 0:wb - t_len, :] = cv_ref[:, t_len:wb, :]
    vout_ref[:, wb - t_len:wb, :] = seq3(va)
    nq = ATT_REP * t_len
    tq_c = lax.broadcasted_iota(jnp.int32, (nb, nq, wb), 1) & (t_len - 1)
    kk_c = lax.broadcasted_iota(jnp.int32, (nb, nq, wb), 2)
    valid_c = kk_c > tq_c
    tq_n = lax.broadcasted_iota(jnp.int32, (nb, nq, t_len), 1) & (t_len - 1)
    kk_n = lax.broadcasted_iota(jnp.int32, (nb, nq, t_len), 2)
    valid_n = kk_n <= tq_n
    rr = lax.broadcasted_iota(jnp.int32, (1, nq, 1), 1) // t_len
    outs = [None] * ATT_Q_HEADS
    for g in range(ATT_KV_HEADS):
        gs = slice(g * HEAD_DIM, (g + 1) * HEAD_DIM)
        kc = ck_ref[:, :, gs].astype(BF16)
        vc = cv_ref[:, :, gs].astype(BF16)
        kn = seq3(_head(ka, g)).astype(BF16)
        vnw = seq3(_head(va, g)).astype(BF16)
        qg = jnp.concatenate([seq3(_head(qa, g * ATT_REP + r)) for r in range(ATT_REP)],
                             axis=1).astype(BF16)
        s_c = jnp.einsum("nqd,nkd->nqk", qg, kc, preferred_element_type=F32) * (HEAD_DIM ** -0.5)
        s_n = jnp.einsum("nqd,nkd->nqk", qg, kn, preferred_element_type=F32) * (HEAD_DIM ** -0.5)
        s_c = jnp.where(valid_c, s_c, NEG)
        s_n = jnp.where(valid_n, s_n, NEG)
        sink = jnp.zeros((1, nq, 1), F32)
        for r in range(ATT_REP):
            sink = jnp.where(rr == r, sinks_ref[g * ATT_REP + r], sink)
        mx = jnp.maximum(jnp.maximum(jnp.max(s_c, axis=-1, keepdims=True),
                                     jnp.max(s_n, axis=-1, keepdims=True)), sink)
        e_c = jnp.exp(s_c - mx)
        e_n = jnp.exp(s_n - mx)
        den = (jnp.sum(e_c, axis=-1, keepdims=True) + jnp.sum(e_n, axis=-1, keepdims=True)
               + jnp.exp(sink - mx))
        o = (jnp.einsum("nqk,nkd->nqd", (e_c / den).astype(BF16), vc, preferred_element_type=F32)
             + jnp.einsum("nqk,nkd->nqd", (e_n / den).astype(BF16), vnw,
                          preferred_element_type=F32))
        for r in range(ATT_REP):
            outs[g * ATT_REP + r] = o[:, r * t_len:(r + 1) * t_len, :].reshape(nb * t_len, HEAD_DIM)
    ocat_ref[:, 1024:1536] = jnp.concatenate(outs, axis=-1).astype(BF16)


def _sample_mixer(proj, s0, ck, cv, tabs, lw):
    ns, wb = ck.shape[0], ck.shape[1]
    nb = SEQ_PER_STEP
    t_len = CHUNK // nb
    const2 = lambda i: (0, 0)
    const3 = lambda i: (0, 0, 0)
    row_blk = lambda i: (i, 0)
    seq_blk3 = lambda i: (i, 0, 0)
    return pl.pallas_call(
        _sample_mixer_kernel,
        grid=(ns // nb,),
        in_specs=[pl.BlockSpec((CHUNK, MIX_WIDTH), row_blk),
                  pl.BlockSpec((nb, RET_HEADS, HEAD_DIM, HEAD_DIM), lambda i: (i, 0, 0, 0)),
                  pl.BlockSpec((nb, wb, 128), seq_blk3),
                  pl.BlockSpec((nb, wb, 128), seq_blk3),
                  pl.BlockSpec((CHUNK, 512), const2),
                  pl.BlockSpec((CHUNK, 512), const2),
                  pl.BlockSpec((GM_GROUPS, CHUNK, CHUNK), const3),
                  pl.BlockSpec((GM_GROUPS, CHUNK, CHUNK), const3),
                  pl.BlockSpec((1, 512), const2),
                  pl.BlockSpec((1, 512), const2),
                  pl.BlockSpec((RET_HEADS, CHUNK, CHUNK), const3),
                  pl.BlockSpec((CHUNK, 512), const2),
                  pl.BlockSpec((CHUNK, 512), const2),
                  pl.BlockSpec((RET_HEADS, HEAD_DIM, HEAD_DIM), const3),
                  pl.BlockSpec((1, 512), const2),
                  pl.BlockSpec(memory_space=pltpu.SMEM)],
        out_specs=[pl.BlockSpec((CHUNK, 3 * 512), row_blk),
                   pl.BlockSpec((nb, RET_HEADS, HEAD_DIM, HEAD_DIM), lambda i: (i, 0, 0, 0)),
                   pl.BlockSpec((nb, wb, 128), seq_blk3),
                   pl.BlockSpec((nb, wb, 128), seq_blk3),
                   pl.BlockSpec((CHUNK, GM_WIDTH), lambda i: (i, 0))],
        out_shape=[jax.ShapeDtypeStruct((ns * t_len, 3 * 512), BF16),
                   jax.ShapeDtypeStruct(s0.shape, F32),
                   jax.ShapeDtypeStruct(ck.shape, F32),
                   jax.ShapeDtypeStruct(cv.shape, F32),
                   jax.ShapeDtypeStruct((ns * t_len, GM_WIDTH), F32)],
        compiler_params=pltpu.CompilerParams(dimension_semantics=("parallel",),
                                             vmem_limit_bytes=VMEM_LIMIT),
        name="sample_mixer",
    )(proj, s0, ck, cv, tabs["cos_s"], tabs["sin_s"], lw["wm_s"], lw["gb_s"],
      lw["ln_w"], lw["ln_b"], tabs["dmat_s"], tabs["qdec_s"], tabs["kdec_s"], tabs["sdec_s"],
      lw["gn_w"], lw["sinks"])


def _gate_out_kernel(x_ref, oc_ref, nw_ref, wg_ref, bg_ref, wbr_ref, wout_ref, h_ref):
    x = x_ref[...]
    xn = _rms(x, nw_ref[...]).astype(BF16)
    m = None
    for b in range(3):
        cols = slice(b * D_MODEL, (b + 1) * D_MODEL)
        gate = jax.nn.sigmoid(jnp.dot(xn, wg_ref[:, cols], preferred_element_type=F32)
                              + bg_ref[:, cols])
        p = jnp.dot(oc_ref[:, b * 512:(b + 1) * 512], wbr_ref[b * 512:(b + 1) * 512, :],
                    preferred_element_type=F32)
        m = gate * p if m is None else m + gate * p
    h_ref[...] = x + jnp.dot(m.astype(BF16), wout_ref[...], preferred_element_type=F32)


def _gate_out(x, ocat, nw, wg, bg, wbr, wout):
    m, d = x.shape
    tm = _tile(m, 512)
    const = lambda i: (0, 0)
    return pl.pallas_call(
        _gate_out_kernel,
        grid=(m // tm,),
        in_specs=[pl.BlockSpec((tm, d), lambda i: (i, 0)),
                  pl.BlockSpec((tm, 3 * 512), lambda i: (i, 0)),
                  pl.BlockSpec((1, d), const),
                  pl.BlockSpec((d, GATE_WIDTH), const),
                  pl.BlockSpec((1, GATE_WIDTH), const),
                  pl.BlockSpec((3 * 512, d), const),
                  pl.BlockSpec((d, d), const)],
        out_specs=pl.BlockSpec((tm, d), lambda i: (i, 0)),
        out_shape=jax.ShapeDtypeStruct((m, d), F32),
        compiler_params=pltpu.CompilerParams(dimension_semantics=("parallel",),
                                             vmem_limit_bytes=VMEM_LIMIT),
        name="gate_out",
    )(x, ocat, nw, wg, bg, wbr, wout)


def _col_max(w):
    return jnp.max(w, axis=0, keepdims=True)


def _top_sorted(w, with_rank):
    row = lax.broadcasted_iota(jnp.int32, (PEER_TOPK, w.shape[1]), 0)
    out = jnp.zeros((PEER_TOPK, w.shape[1]), F32)
    rank = jnp.full(w.shape, float(PEER_TOPK), F32)
    for a in range(PEER_TOPK):
        m = _col_max(w)
        out = jnp.where(row == a, m, out)
        top = w == m
        if with_rank:
            rank = jnp.where(top, float(a), rank)
        if a + 1 < PEER_TOPK:
            w = jnp.where(top, -jnp.inf, w)
    return out, rank


def _select_counts(v1, v2):
    starts, blocks = [0], [v2 + v1[0:1]]
    for a in range(1, 8):
        width = 8 if a < 4 else 4
        starts.append(starts[-1] + blocks[-1].shape[0])
        blocks.append(v2[0:width] + v1[a:a + 1])
    starts.append(starts[-1] + blocks[-1].shape[0])
    blocks.append(v1[8:16] + v2[0:1])
    cand = jnp.concatenate(blocks, axis=0)
    cell = lax.broadcasted_iota(jnp.int32, cand.shape, 0)
    w = cand
    ones = jnp.zeros_like(cand)
    for _ in range(PEER_TOPK):
        first = jnp.min(jnp.where(w == _col_max(w), cell, cand.shape[0]), axis=0, keepdims=True)
        pick = cell == first
        ones = jnp.where(pick, 1.0, ones)
        w = jnp.where(pick, -jnp.inf, w)
    e = jnp.where(ones > 0.0, jnp.exp(cand - (v1[0:1] + v2[0:1])), 0.0)
    row = lax.broadcasted_iota(jnp.int32, (8, v1.shape[1]), 0)
    low = jnp.zeros((8, v1.shape[1]), F32)
    for a in range(8):
        low = jnp.where(row == a, jnp.sum(ones[starts[a]:starts[a + 1]], axis=0, keepdims=True),
                        low)
    counts = jnp.concatenate([low, ones[starts[8]:starts[8] + 8]], axis=0)
    return counts, 1.0 / jnp.sum(e, axis=0, keepdims=True)


def _peer_kernel(*refs, final_norm):
    if final_norm:
        (h_ref, nw_ref, wqt_ref, k1_ref, k2_ref, pu_ref, pvt_ref, nf_ref, out_ref,
         xnt_scr, cnt_scr, a_scr, rank_scr, b_scr, z_scr, acc_scr) = refs
    else:
        (h_ref, nw_ref, wqt_ref, k1_ref, k2_ref, pu_ref, pvt_ref, out_ref,
         xnt_scr, cnt_scr, a_scr, rank_scr, b_scr, z_scr, acc_scr) = refs
        nf_ref = None
    j = pl.program_id(1)
    tm = h_ref.shape[0]
    nlc = tm // LANES
    nsub = pu_ref.shape[0] // PEER_NKEYS
    dq = 2 * PEER_NKEYS
    quarter = PEER_QUARTER_KEYS * PEER_NKEYS

    @pl.when(j == 0)
    def _prepare():
        xn = _rms(h_ref[...], nw_ref[...])
        xnt = xn.T.astype(BF16)
        xnt_scr[...] = xnt
        qt = jnp.dot(wqt_ref[...], xnt, preferred_element_type=F32).astype(BF16)
        for h in range(PEER_HEADS):
            s1 = jnp.dot(k1_ref[...], qt[h * dq:h * dq + PEER_NKEYS], preferred_element_type=F32)
            s2 = jnp.dot(k2_ref[...], qt[h * dq + PEER_NKEYS:(h + 1) * dq],
                         preferred_element_type=F32)
            for lc in range(nlc):
                cs = slice(lc * LANES, (lc + 1) * LANES)
                s1c = s1[:, cs]
                s2c = s2[:, cs]
                v1, _ = _top_sorted(s1c, False)
                v2, rank2 = _top_sorted(s2c, True)
                counts, zinv = _select_counts(v1, v2)
                last = jnp.min(jnp.where(counts[8:16] > 0.0, v1[8:16], jnp.inf),
                               axis=0, keepdims=True)
                cnt = jnp.where(s1c >= last, jnp.where(s1c < v1[7:8], 1.0, 0.0), 0.0)
                for a in range(8):
                    cnt = jnp.where(s1c == v1[a:a + 1], counts[a:a + 1], cnt)
                cnt_scr[lc, pl.ds(h, PEER_NKEYS, stride=PEER_HEADS), :] = cnt
                a_scr[lc, pl.ds(h, PEER_NKEYS, stride=PEER_HEADS), :] = jnp.exp(s1c - v1[0:1])
                rank_scr[h, :, cs] = rank2
                b_scr[h, :, cs] = jnp.exp(s2c - v2[0:1]) * zinv
        acc_scr[...] = jnp.zeros_like(acc_scr)

    base = pl.multiple_of(j * nsub, nsub)

    def activations(q):
        return jnp.dot(pu_ref[q * quarter:(q + 1) * quarter, :], xnt_scr[...],
                       preferred_element_type=F32)

    def weigh(act, q):
        for lc in range(nlc):
            cs = slice(lc * LANES, (lc + 1) * LANES)
            keys = [q * PEER_QUARTER_KEYS + ii for ii in range(PEER_QUARTER_KEYS)]
            heads = [pl.ds(pl.multiple_of((base + r) * PEER_HEADS, PEER_HEADS), PEER_HEADS) for r in keys]
            crows = [cnt_scr[lc, hs, :] for hs in heads]
            arows = [a_scr[lc, hs, :] for hs in heads]
            ws = [jnp.zeros((PEER_NKEYS, LANES), F32) for _ in keys]
            for h in range(PEER_HEADS):
                rank = rank_scr[h, :, cs]
                b = b_scr[h, :, cs]
                for n, r in enumerate(keys):
                    hit = rank < crows[n][h:h + 1]
                    ws[n] = ws[n] + jnp.where(hit, b, 0.0) * arows[n][h:h + 1]
            for n, r in enumerate(keys):
                g = _gelu(act[n * PEER_NKEYS:(n + 1) * PEER_NKEYS, cs])
                z_scr[r * PEER_NKEYS:(r + 1) * PEER_NKEYS, cs] = (ws[n] * g).astype(BF16)

    for q in range(nsub // PEER_QUARTER_KEYS):
        weigh(activations(q), q)
    acc_scr[...] += jnp.dot(pvt_ref[...], z_scr[...], preferred_element_type=F32)

    @pl.when(j == pl.num_programs(1) - 1)
    def _finish():
        y = h_ref[...] + acc_scr[...].T
        if final_norm:
            y = _rms(y, nf_ref[...])
        out_ref[...] = y


def _peer(h, nw, wqt, k1, k2, pu, pvt, nf, layer):
    m, d = h.shape
    ne = pu.shape[1]
    tm = _tile(m, 512)
    eb = PEER_EXPERT_BLOCK
    assert eb == 8 * PEER_NKEYS and ne % eb == 0
    const = lambda i, j: (0, 0)
    in_specs = [pl.BlockSpec((tm, d), lambda i, j: (i, 0)),
                pl.BlockSpec((1, d), const),
                pl.BlockSpec((None,) + wqt.shape[1:], lambda i, j: (layer, 0, 0)),
                pl.BlockSpec(k1.shape, const),
                pl.BlockSpec(k2.shape, const),
                pl.BlockSpec((None, eb, d), lambda i, j: (layer, j, 0)),
                pl.BlockSpec((None, d, eb), lambda i, j: (layer, 0, j))]
    args = [h, nw, wqt, k1, k2, pu, pvt]
    if nf is not None:
        in_specs.append(pl.BlockSpec((1, d), const))
        args.append(nf)
    sel = pltpu.VMEM((PEER_HEADS, PEER_NKEYS, tm), F32)
    sel_kh = pltpu.VMEM((tm // LANES, PEER_NKEYS * PEER_HEADS, LANES), F32)
    return pl.pallas_call(
        functools.partial(_peer_kernel, final_norm=nf is not None),
        grid=(m // tm, ne // eb),
        in_specs=in_specs,
        out_specs=pl.BlockSpec((tm, d), lambda i, j: (i, 0)),
        out_shape=jax.ShapeDtypeStruct((m, d), F32),
        scratch_shapes=[pltpu.VMEM((d, tm), BF16), sel_kh, sel_kh, sel, sel,
                        pltpu.VMEM((eb, tm), BF16),
                        pltpu.VMEM((d, tm), F32)],
        compiler_params=pltpu.CompilerParams(dimension_semantics=("parallel", "arbitrary"),
                                             vmem_limit_bytes=VMEM_LIMIT),
        name="peer",
    )(*args)


def _rope_tables(pos):
    half = HEAD_DIM // 2
    inv = jnp.exp(-math.log(ROPE_THETA) * jnp.arange(half, dtype=F32) / half)
    ang = pos.astype(F32)[:, None] * inv[None, :]
    cos, sin = jnp.cos(ang), jnp.sin(ang)
    return (jnp.tile(jnp.concatenate([cos, cos], axis=-1), (1, RET_HEADS)),
            jnp.tile(jnp.concatenate([-sin, sin], axis=-1), (1, RET_HEADS)))


def _decay_tables(t_idx, same_seq, c_len):
    lg = jnp.log1p(-jnp.exp2(-5.0 - jnp.arange(RET_HEADS, dtype=F32)))[:, None]
    t = t_idx.astype(F32)
    diff = t[:, None] - t[None, :]
    dmat = jnp.where((diff[None] >= 0) & same_seq[None],
                     jnp.exp(jnp.maximum(diff, 0.0)[None] * lg[:, :, None]), 0.0)
    qdec = jnp.repeat(jnp.exp((t + 1.0)[None] * lg).T, HEAD_DIM, axis=1)
    kdec = jnp.repeat(jnp.exp((c_len - 1.0 - t)[None] * lg).T, HEAD_DIM, axis=1)
    sdec = jnp.broadcast_to(jnp.exp(c_len * lg)[:, :, None], (RET_HEADS, HEAD_DIM, HEAD_DIM))
    return dmat, qdec, kdec, sdec


def kernel(x_prompt, x_sample, state_ret, cache_win_k, cache_win_v, norm1_w, norm2_w, normf_w, w_in, b_gate, gm_ln_w, gm_ln_b, gm_ws, gm_b, ret_gn_w, attn_sinks, w_br_a, w_br_b, w_br_c, w_out, peer_wq, peer_k1, peer_k2, peer_u, peer_v):
    batch, seq, d = x_prompt.shape
    ns, t_len, _ = x_sample.shape
    depth = w_in.shape[0]
    wb = cache_win_k.shape[2]
    assert d == D_MODEL and seq % CHUNK == 0 and t_len * SEQ_PER_STEP == CHUNK
    assert ns % SEQ_PER_STEP == 0 and wb == CHUNK
    xp = x_prompt.reshape(batch * seq, d)
    xs = x_sample.reshape(ns * t_len, d)

    row = jnp.arange(CHUNK)
    tabs = {}
    tabs["cos_p"], tabs["sin_p"] = _rope_tables(jnp.arange(seq))
    tabs["cos_s"], tabs["sin_s"] = _rope_tables(PAST_LEN + row % t_len)
    (tabs["dmat_p"], tabs["qdec_p"], tabs["kdec_p"], tabs["sdec_p"]) = _decay_tables(
        row, jnp.ones((CHUNK, CHUNK), bool), float(CHUNK))
    (tabs["dmat_s"], tabs["qdec_s"], tabs["kdec_s"], tabs["sdec_s"]) = _decay_tables(
        row % t_len, (row // t_len)[:, None] == (row // t_len)[None, :], float(t_len))
    causal = jnp.tril(jnp.ones((CHUNK, CHUNK), bool))
    eye = jnp.eye(SEQ_PER_STEP, dtype=F32)

    w_in_all = w_in.astype(BF16)
    wqt_all = jnp.swapaxes(peer_wq, 1, 2).astype(BF16)
    pu_all = peer_u.astype(BF16)
    pvt_all = jnp.swapaxes(peer_v, 1, 2).astype(BF16)

    s_p, s_s, k_p, v_p, k_s, v_s, g_s = [], [], [], [], [], [], []
    for l in range(depth):
        ws = gm_ws[l]
        causal_s = jnp.tril(jnp.ones((t_len, t_len), bool))
        ws_s = jnp.where(causal_s[None], ws[:, :t_len, :t_len], 0.0)
        lw = {
            "wm_p": jnp.where(causal[None], ws, 0.0).astype(BF16),
            "gb_p": jnp.broadcast_to(gm_b[l][:, :, None], (GM_GROUPS, CHUNK, CHUNK)),
            "wm_s": jnp.stack([jnp.kron(eye, ws_s[g]) for g in range(GM_GROUPS)]).astype(BF16),
            "gb_s": jnp.broadcast_to(jnp.tile(gm_b[l][:, :t_len], (1, SEQ_PER_STEP))[:, :, None],
                                     (GM_GROUPS, CHUNK, CHUNK)),
            "ln_w": gm_ln_w[l][None], "ln_b": gm_ln_b[l][None],
            "gn_w": ret_gn_w[l][None], "sinks": attn_sinks[l],
        }
        n1 = norm1_w[l][None]
        proj_p = _in_proj(xp, n1, w_in_all, l, MIX_WIDTH)
        proj_s = _in_proj(xs, n1, w_in_all, l, MIX_WIDTH)
        ocat_p, sp, kp, vp = _prompt_mixer(proj_p, batch, seq, tabs, lw)
        ocat_s, ss, ks, vs, gs = _sample_mixer(
            proj_s, state_ret[l], cache_win_k[l].reshape(ns, wb, 128),
            cache_win_v[l].reshape(ns, wb, 128), tabs, lw)
        gate_w = (n1, w_in_all[l, :, MIX_WIDTH:], b_gate[l][None],
                  jnp.concatenate([w_br_a[l], w_br_b[l], w_br_c[l]], axis=0).astype(BF16),
                  w_out[l].astype(BF16))
        peer_w = (norm2_w[l][None], wqt_all, peer_k1[l].astype(BF16), peer_k2[l].astype(BF16),
                  pu_all, pvt_all, normf_w[None] if l == depth - 1 else None, l)
        xp = _peer(_gate_out(xp, ocat_p, *gate_w), *peer_w)
        xs = _peer(_gate_out(xs, ocat_s, *gate_w), *peer_w)
        s_p.append(sp)
        s_s.append(ss)
        k_p.append(kp.reshape(batch, CHUNK, ATT_KV_HEADS, HEAD_DIM))
        v_p.append(vp.reshape(batch, CHUNK, ATT_KV_HEADS, HEAD_DIM))
        k_s.append(ks.reshape(ns, wb, ATT_KV_HEADS, HEAD_DIM))
        v_s.append(vs.reshape(ns, wb, ATT_KV_HEADS, HEAD_DIM))
        g_s.append(gs.reshape(ns, t_len, GM_WIDTH))
    return (xp.reshape(batch, seq, d), xs.reshape(ns, t_len, d),
            jnp.stack(s_p), jnp.stack(s_s), jnp.stack(k_p), jnp.stack(v_p),
            jnp.stack(k_s), jnp.stack(v_s), jnp.stack(g_s))
```

```python
import functools
import math

import numpy as np
import jax
import jax.numpy as jnp
from jax import lax
from jax.experimental import pallas as pl
from jax.experimental.pallas import tpu as pltpu

F32 = jnp.float32
BF16 = jnp.bfloat16

D_MODEL = 1024
HEAD_DIM = 64
CHUNK = 128
GM_WIDTH = 512
GM_GROUPS = 4
RET_HEADS = 8
ATT_Q_HEADS = 8
ATT_KV_HEADS = 2
ATT_REP = 4
ROPE_THETA = 10000.0
PAST_LEN = 16384
PEER_HEADS = 8
PEER_NKEYS = 128
PEER_TOPK = 16
EPS = 1e-6
NEG = -1e30
MIX_WIDTH = 3840
GATE_WIDTH = 3 * D_MODEL
SEQ_PER_STEP = 16
LANES = 128
PEER_EXPERT_BLOCK = 1024
PEER_QUARTER_KEYS = 2
VMEM_LIMIT = 56 * 1024 * 1024


def _tile(m, target):
    t = min(m, target)
    while m % t or t % LANES:
        t -= LANES
    return t


def _rms(x, w):
    return x * lax.rsqrt(jnp.mean(x * x, axis=-1, keepdims=True) + EPS) * w


def _gelu2(x):
    return x * (1.0 + lax.erf(x * np.float32(math.sqrt(0.5))))


def _gelu(x):
    return 0.5 * _gelu2(x)


def _rope(x, cos, sin_signed):
    w = x.shape[-1]
    lane = lax.broadcasted_iota(jnp.int32, x.shape, 1)
    first = (lane & (HEAD_DIM - 1)) < HEAD_DIM // 2
    rot = jnp.where(first, pltpu.roll(x, w - HEAD_DIM // 2, 1), pltpu.roll(x, HEAD_DIM // 2, 1))
    return x * cos + rot * sin_signed


def _head(x, h):
    return x[:, h * HEAD_DIM:(h + 1) * HEAD_DIM]


def _gmlp(gu, gv, lnw, lnb, wm_ref, gb_ref):
    u = _gelu(gu)
    vf = _gelu(gv)
    mu = jnp.mean(vf, axis=-1, keepdims=True)
    var = jnp.mean(jnp.square(vf - mu), axis=-1, keepdims=True)
    vn = (vf - mu) * lax.rsqrt(var + EPS) * lnw + lnb
    vnb = vn.astype(BF16)
    parts = []
    for g in range(GM_GROUPS):
        s = jnp.dot(wm_ref[g], vnb[:, g * LANES:(g + 1) * LANES], preferred_element_type=F32)
        parts.append(s + gb_ref[g])
    return u * jnp.concatenate(parts, axis=-1), vn


def _group_norm(o):
    mu = jnp.mean(o, axis=-1, keepdims=True)
    var = jnp.mean(jnp.square(o - mu), axis=-1, keepdims=True)
    return (o - mu) * lax.rsqrt(var + EPS)


def _dot_nt(a, b):
    return lax.dot_general(a, b, (((1,), (1,)), ((), ())), preferred_element_type=F32)


def _in_proj_kernel(x_ref, nw_ref, w_ref, o_ref):
    xn = _rms(x_ref[...], nw_ref[...]).astype(BF16)
    o_ref[...] = jnp.dot(xn, w_ref[...], preferred_element_type=F32)


def _in_proj(x, nw, w_all, layer, n):
    m, d = x.shape
    tm = _tile(m, 512)
    return pl.pallas_call(
        _in_proj_kernel,
        grid=(m // tm,),
        in_specs=[pl.BlockSpec((tm, d), lambda i: (i, 0)),
                  pl.BlockSpec((1, d), lambda i: (0, 0)),
                  pl.BlockSpec((None, d, n), lambda i: (layer, 0, 0))],
        out_specs=pl.BlockSpec((tm, n), lambda i: (i, 0)),
        out_shape=jax.ShapeDtypeStruct((m, n), F32),
        compiler_params=pltpu.CompilerParams(dimension_semantics=("parallel",),
                                             vmem_limit_bytes=VMEM_LIMIT),
        name="in_proj",
    )(x, nw, w_all)


def _prompt_mixer_kernel(proj_ref, cos_ref, sin_ref, wm_ref, gb_ref, lnw_ref, lnb_ref,
                         dmat_ref, qdec_ref, kdec_ref, sdec_ref, gnw_ref, sinks_ref,
                         ocat_ref, sfin_ref, kout_ref, vout_ref,
                         s_scr, kprev_scr, vprev_scr):
    c = pl.program_id(1)

    @pl.when(c == 0)
    def _reset():
        s_scr[...] = jnp.zeros_like(s_scr)
        kprev_scr[...] = jnp.zeros_like(kprev_scr)
        vprev_scr[...] = jnp.zeros_like(vprev_scr)

    cos = cos_ref[...]
    sin = sin_ref[...]

    o_a, _ = _gmlp(proj_ref[:, 0:512], proj_ref[:, 512:1024], lnw_ref[...], lnb_ref[...],
                   wm_ref, gb_ref)
    ocat_ref[:, 0:512] = o_a.astype(BF16)

    q = _rope(proj_ref[:, 1024:1536], cos, sin)
    ks = _rope(proj_ref[:, 1536:2048], cos, sin) * (HEAD_DIM ** -0.5)
    v = proj_ref[:, 2048:2560]
    qd = q * qdec_ref[...]
    kd = ks * kdec_ref[...]
    ys = []
    for h in range(RET_HEADS):
        qh = _head(q, h).astype(BF16)
        kh = _head(ks, h).astype(BF16)
        vh = _head(v, h).astype(BF16)
        sc = _dot_nt(qh, kh) * dmat_ref[h]
        o = jnp.dot(sc.astype(BF16), vh, preferred_element_type=F32)
        s_old = s_scr[h]
        o = o + jnp.dot(_head(qd, h).astype(BF16), s_old.astype(BF16), preferred_element_type=F32)
        kdh = _head(kd, h).astype(BF16)
        s_new = s_old * sdec_ref[h] + lax.dot_general(
            kdh, vh, (((0,), (0,)), ((), ())), preferred_element_type=F32)
        s_scr[h] = s_new
        sfin_ref[h] = s_new
        ys.append(_group_norm(o))
    y = jnp.concatenate(ys, axis=-1) * gnw_ref[...]
    rg = proj_ref[:, 2560:3072]
    ocat_ref[:, 512:1024] = (rg * jax.nn.sigmoid(rg) * y).astype(BF16)

    qa = _rope(proj_ref[:, 3072:3584], cos, sin)
    ka = _rope(proj_ref[:, 3584:3712], cos[:, 0:128], sin[:, 0:128])
    va = proj_ref[:, 3712:3840]
    kout_ref[...] = ka
    vout_ref[...] = va
    rows = ATT_REP * CHUNK
    tt = lax.broadcasted_iota(jnp.int32, (rows, 2 * CHUNK), 0) & (CHUNK - 1)
    kk = lax.broadcasted_iota(jnp.int32, (rows, 2 * CHUNK), 1)
    valid = (kk > tt) & (kk <= tt + CHUNK) & ((kk >= CHUNK) | (c > 0))
    rr = lax.broadcasted_iota(jnp.int32, (rows, 1), 0) // CHUNK
    outs = [None] * ATT_Q_HEADS
    for g in range(ATT_KV_HEADS):
        kcat = jnp.concatenate([_head(kprev_scr[...], g), _head(ka, g)], axis=0).astype(BF16)
        vcat = jnp.concatenate([_head(vprev_scr[...], g), _head(va, g)], axis=0).astype(BF16)
        qg = jnp.concatenate([_head(qa, g * ATT_REP + r) for r in range(ATT_REP)],
                             axis=0).astype(BF16)
        s = _dot_nt(qg, kcat) * (HEAD_DIM ** -0.5)
        s = jnp.where(valid, s, NEG)
        sink = jnp.zeros((rows, 1), F32)
        for r in range(ATT_REP):
            sink = jnp.where(rr == r, sinks_ref[g * ATT_REP + r], sink)
        mx = jnp.maximum(jnp.max(s, axis=-1, keepdims=True), sink)
        e = jnp.exp(s - mx)
        den = jnp.sum(e, axis=-1, keepdims=True) + jnp.exp(sink - mx)
        p = (e / den).astype(BF16)
        o = jnp.dot(p, vcat, preferred_element_type=F32)
        for r in range(ATT_REP):
            outs[g * ATT_REP + r] = o[r * CHUNK:(r + 1) * CHUNK]
    ocat_ref[:, 1024:1536] = jnp.concatenate(outs, axis=-1).astype(BF16)
    kprev_scr[...] = ka
    vprev_scr[...] = va


def _prompt_mixer(proj, batch, seq, tabs, lw):
    nchunk = seq // CHUNK
    const2 = lambda b, c: (0, 0)
    const3 = lambda b, c: (0, 0, 0)
    row_blk = lambda b, c: (b * nchunk + c, 0)
    return pl.pallas_call(
        _prompt_mixer_kernel,
        grid=(batch, nchunk),
        in_specs=[pl.BlockSpec((CHUNK, MIX_WIDTH), row_blk),
                  pl.BlockSpec((CHUNK, 512), lambda b, c: (c, 0)),
                  pl.BlockSpec((CHUNK, 512), lambda b, c: (c, 0)),
                  pl.BlockSpec((GM_GROUPS, CHUNK, CHUNK), const3),
                  pl.BlockSpec((GM_GROUPS, CHUNK, CHUNK), const3),
                  pl.BlockSpec((1, 512), const2),
                  pl.BlockSpec((1, 512), const2),
                  pl.BlockSpec((RET_HEADS, CHUNK, CHUNK), const3),
                  pl.BlockSpec((CHUNK, 512), const2),
                  pl.BlockSpec((CHUNK, 512), const2),
                  pl.BlockSpec((RET_HEADS, HEAD_DIM, HEAD_DIM), const3),
                  pl.BlockSpec((1, 512), const2),
                  pl.BlockSpec(memory_space=pltpu.SMEM)],
        out_specs=[pl.BlockSpec((CHUNK, 3 * 512), row_blk),
                   pl.BlockSpec((None, RET_HEADS, HEAD_DIM, HEAD_DIM), lambda b, c: (b, 0, 0, 0)),
                   pl.BlockSpec((None, CHUNK, 128), lambda b, c: (b, 0, 0)),
                   pl.BlockSpec((None, CHUNK, 128), lambda b, c: (b, 0, 0))],
        out_shape=[jax.ShapeDtypeStruct((batch * seq, 3 * 512), BF16),
                   jax.ShapeDtypeStruct((batch, RET_HEADS, HEAD_DIM, HEAD_DIM), F32),
                   jax.ShapeDtypeStruct((batch, CHUNK, 128), F32),
                   jax.ShapeDtypeStruct((batch, CHUNK, 128), F32)],
        scratch_shapes=[pltpu.VMEM((RET_HEADS, HEAD_DIM, HEAD_DIM), F32),
                        pltpu.VMEM((CHUNK, 128), F32),
                        pltpu.VMEM((CHUNK, 128), F32)],
        compiler_params=pltpu.CompilerParams(dimension_semantics=("parallel", "arbitrary"),
                                             vmem_limit_bytes=VMEM_LIMIT),
        name="prompt_mixer",
    )(proj, tabs["cos_p"], tabs["sin_p"], lw["wm_p"], lw["gb_p"], lw["ln_w"], lw["ln_b"],
      tabs["dmat_p"], tabs["qdec_p"], tabs["kdec_p"], tabs["sdec_p"], lw["gn_w"], lw["sinks"])


def _sample_mixer_kernel(proj_ref, s0_ref, ck_ref, cv_ref, cos_ref, sin_ref,
                         wm_ref, gb_ref, lnw_ref, lnb_ref, dmat_ref, qdec_ref, kdec_ref,
                         sdec_ref, gnw_ref, sinks_ref,
                         ocat_ref, snew_ref, kout_ref, vout_ref, vn_ref):
    nb = s0_ref.shape[0]
    t_len = proj_ref.shape[0] // nb
    cos = cos_ref[...]
    sin = sin_ref[...]

    o_a, vn = _gmlp(proj_ref[:, 0:512], proj_ref[:, 512:1024], lnw_ref[...], lnb_ref[...],
                    wm_ref, gb_ref)
    ocat_ref[:, 0:512] = o_a.astype(BF16)
    vn_ref[...] = vn

    def seq3(x):
        return x.reshape(nb, t_len, x.shape[-1])

    q = _rope(proj_ref[:, 1024:1536], cos, sin)
    ks = _rope(proj_ref[:, 1536:2048], cos, sin) * (HEAD_DIM ** -0.5)
    v = proj_ref[:, 2048:2560]
    qd = q * qdec_ref[...]
    kd = ks * kdec_ref[...]
    ys = []
    for h in range(RET_HEADS):
        qh = _head(q, h).astype(BF16)
        kh = _head(ks, h).astype(BF16)
        vh = _head(v, h).astype(BF16)
        sc = _dot_nt(qh, kh) * dmat_ref[h]
        o = jnp.dot(sc.astype(BF16), vh, preferred_element_type=F32)
        s_old = s0_ref[:, h]
        oc = jnp.einsum("ntd,nde->nte", seq3(_head(qd, h)).astype(BF16), s_old.astype(BF16),
                        preferred_element_type=F32)
        o = o + oc.reshape(nb * t_len, HEAD_DIM)
        kd3t = jnp.swapaxes(seq3(_head(kd, h)), 1, 2).astype(BF16)
        upd = jnp.einsum("ndt,nte->nde", kd3t, seq3(_head(v, h)).astype(BF16),
                         preferred_element_type=F32)
        snew_ref[:, h] = s_old * sdec_ref[h] + upd
        ys.append(_group_norm(o))
    y = jnp.concatenate(ys, axis=-1) * gnw_ref[...]
    rg = proj_ref[:, 2560:3072]
    ocat_ref[:, 512:1024] = (rg * jax.nn.sigmoid(rg) * y).astype(BF16)

    qa = _rope(proj_ref[:, 3072:3584], cos, sin)
    ka = _rope(proj_ref[:, 3584:3712], cos[:, 0:128], sin[:, 0:128])
    va = proj_ref[:, 3712:3840]
    wb = ck_ref.shape[1]
    kout_ref[:, 0:wb - t_len, :] = ck_ref[:, t_len:wb, :]
    kout_ref[:, wb - t_len:wb, :] = seq3(ka)
    vout_ref[:, 0:wb - t_len, :] = cv_ref[:, t_len:wb, :]
    vout_ref[:, wb - t_len:wb, :] = seq3(va)
    nq = ATT_REP * t_len
    tq_c = lax.broadcasted_iota(jnp.int32, (nb, nq, wb), 1) & (t_len - 1)
    kk_c = lax.broadcasted_iota(jnp.int32, (nb, nq, wb), 2)
    valid_c = kk_c > tq_c
    tq_n = lax.broadcasted_iota(jnp.int32, (nb, nq, t_len), 1) & (t_len - 1)
    kk_n = lax.broadcasted_iota(jnp.int32, (nb, nq, t_len), 2)
    valid_n = kk_n <= tq_n
    rr = lax.broadcasted_iota(jnp.int32, (1, nq, 1), 1) // t_len
    outs = [None] * ATT_Q_HEADS
    for g in range(ATT_KV_HEADS):
        gs = slice(g * HEAD_DIM, (g + 1) * HEAD_DIM)
        kc = ck_ref[:, :, gs].astype(BF16)
        vc = cv_ref[:, :, gs].astype(BF16)
        kn = seq3(_head(ka, g)).astype(BF16)
        vnw = seq3(_head(va, g)).astype(BF16)
        qg = jnp.concatenate([seq3(_head(qa, g * ATT_REP + r)) for r in range(ATT_REP)],
                             axis=1).astype(BF16)
        s_c = jnp.einsum("nqd,nkd->nqk", qg, kc, preferred_element_type=F32) * (HEAD_DIM ** -0.5)
        s_n = jnp.einsum("nqd,nkd->nqk", qg, kn, preferred_element_type=F32) * (HEAD_DIM ** -0.5)
        s_c = jnp.where(valid_c, s_c, NEG)
        s_n = jnp.where(valid_n, s_n, NEG)
        sink = jnp.zeros((1, nq, 1), F32)
        for r in range(ATT_REP):
            sink = jnp.where(rr == r, sinks_ref[g * ATT_REP + r], sink)
        mx = jnp.maximum(jnp.maximum(jnp.max(s_c, axis=-1, keepdims=True),
                                     jnp.max(s_n, axis=-1, keepdims=True)), sink)
        e_c = jnp.exp(s_c - mx)
        e_n = jnp.exp(s_n - mx)
        den = (jnp.sum(e_c, axis=-1, keepdims=True) + jnp.sum(e_n, axis=-1, keepdims=True)
               + jnp.exp(sink - mx))
        o = (jnp.einsum("nqk,nkd->nqd", (e_c / den).astype(BF16), vc, preferred_element_type=F32)
             + jnp.einsum("nqk,nkd->nqd", (e_n / den).astype(BF16), vnw,
                          preferred_element_type=F32))
        for r in range(ATT_REP):
            outs[g * ATT_REP + r] = o[:, r * t_len:(r + 1) * t_len, :].reshape(nb * t_len, HEAD_DIM)
    ocat_ref[:, 1024:1536] = jnp.concatenate(outs, axis=-1).astype(BF16)


def _sample_mixer(proj, s0, ck, cv, tabs, lw):
    ns, wb = ck.shape[0], ck.shape[1]
    nb = SEQ_PER_STEP
    t_len = CHUNK // nb
    const2 = lambda i: (0, 0)
    const3 = lambda i: (0, 0, 0)
    row_blk = lambda i: (i, 0)
    seq_blk3 = lambda i: (i, 0, 0)
    return pl.pallas_call(
        _sample_mixer_kernel,
        grid=(ns // nb,),
        in_specs=[pl.BlockSpec((CHUNK, MIX_WIDTH), row_blk),
                  pl.BlockSpec((nb, RET_HEADS, HEAD_DIM, HEAD_DIM), lambda i: (i, 0, 0, 0)),
                  pl.BlockSpec((nb, wb, 128), seq_blk3),
                  pl.BlockSpec((nb, wb, 128), seq_blk3),
                  pl.BlockSpec((CHUNK, 512), const2),
                  pl.BlockSpec((CHUNK, 512), const2),
                  pl.BlockSpec((GM_GROUPS, CHUNK, CHUNK), const3),
                  pl.BlockSpec((GM_GROUPS, CHUNK, CHUNK), const3),
                  pl.BlockSpec((1, 512), const2),
                  pl.BlockSpec((1, 512), const2),
                  pl.BlockSpec((RET_HEADS, CHUNK, CHUNK), const3),
                  pl.BlockSpec((CHUNK, 512), const2),
                  pl.BlockSpec((CHUNK, 512), const2),
                  pl.BlockSpec((RET_HEADS, HEAD_DIM, HEAD_DIM), const3),
                  pl.BlockSpec((1, 512), const2),
                  pl.BlockSpec(memory_space=pltpu.SMEM)],
        out_specs=[pl.BlockSpec((CHUNK, 3 * 512), row_blk),
                   pl.BlockSpec((nb, RET_HEADS, HEAD_DIM, HEAD_DIM), lambda i: (i, 0, 0, 0)),
                   pl.BlockSpec((nb, wb, 128), seq_blk3),
                   pl.BlockSpec((nb, wb, 128), seq_blk3),
                   pl.BlockSpec((CHUNK, GM_WIDTH), lambda i: (i, 0))],
        out_shape=[jax.ShapeDtypeStruct((ns * t_len, 3 * 512), BF16),
                   jax.ShapeDtypeStruct(s0.shape, F32),
                   jax.ShapeDtypeStruct(ck.shape, F32),
                   jax.ShapeDtypeStruct(cv.shape, F32),
                   jax.ShapeDtypeStruct((ns * t_len, GM_WIDTH), F32)],
        compiler_params=pltpu.CompilerParams(dimension_semantics=("parallel",),
                                             vmem_limit_bytes=VMEM_LIMIT),
        name="sample_mixer",
    )(proj, s0, ck, cv, tabs["cos_s"], tabs["sin_s"], lw["wm_s"], lw["gb_s"],
      lw["ln_w"], lw["ln_b"], tabs["dmat_s"], tabs["qdec_s"], tabs["kdec_s"], tabs["sdec_s"],
      lw["gn_w"], lw["sinks"])


def _gate_out_kernel(x_ref, oc_ref, nw_ref, wg_ref, bg_ref, wbr_ref, wout_ref, h_ref):
    x = x_ref[...]
    xn = _rms(x, nw_ref[...]).astype(BF16)
    m = None
    for b in range(3):
        cols = slice(b * D_MODEL, (b + 1) * D_MODEL)
        gate = jax.nn.sigmoid(jnp.dot(xn, wg_ref[:, cols], preferred_element_type=F32)
                              + bg_ref[:, cols])
        p = jnp.dot(oc_ref[:, b * 512:(b + 1) * 512], wbr_ref[b * 512:(b + 1) * 512, :],
                    preferred_element_type=F32)
        m = gate * p if m is None else m + gate * p
    h_ref[...] = x + jnp.dot(m.astype(BF16), wout_ref[...], preferred_element_type=F32)


def _gate_out(x, ocat, nw, wg, bg, wbr, wout):
    m, d = x.shape
    tm = _tile(m, 512)
    const = lambda i: (0, 0)
    return pl.pallas_call(
        _gate_out_kernel,
        grid=(m // tm,),
        in_specs=[pl.BlockSpec((tm, d), lambda i: (i, 0)),
                  pl.BlockSpec((tm, 3 * 512), lambda i: (i, 0)),
                  pl.BlockSpec((1, d), const),
                  pl.BlockSpec((d, GATE_WIDTH), const),
                  pl.BlockSpec((1, GATE_WIDTH), const),
                  pl.BlockSpec((3 * 512, d), const),
                  pl.BlockSpec((d, d), const)],
        out_specs=pl.BlockSpec((tm, d), lambda i: (i, 0)),
        out_shape=jax.ShapeDtypeStruct((m, d), F32),
        compiler_params=pltpu.CompilerParams(dimension_semantics=("parallel",),
                                             vmem_limit_bytes=VMEM_LIMIT),
        name="gate_out",
    )(x, ocat, nw, wg, bg, wbr, wout)


def _col_max(w):
    return jnp.max(w, axis=0, keepdims=True)


def _top_sorted(w, with_rank):
    row = lax.broadcasted_iota(jnp.int32, (PEER_TOPK, w.shape[1]), 0)
    out = jnp.zeros((PEER_TOPK, w.shape[1]), F32)
    rank = jnp.full(w.shape, float(PEER_TOPK), F32)
    for a in range(PEER_TOPK):
        m = _col_max(w)
        out = jnp.where(row == a, m, out)
        top = w == m
        if with_rank:
            rank = jnp.where(top, float(a), rank)
        if a + 1 < PEER_TOPK:
            w = jnp.where(top, -jnp.inf, w)
    return out, rank


def _select_counts(v1, v2):
    starts, blocks = [0], [v2 + v1[0:1]]
    for a in range(1, 8):
        width = 8 if a < 4 else 4
        starts.append(starts[-1] + blocks[-1].shape[0])
        blocks.append(v2[0:width] + v1[a:a + 1])
    starts.append(starts[-1] + blocks[-1].shape[0])
    blocks.append(v1[8:16] + v2[0:1])
    cand = jnp.concatenate(blocks, axis=0)
    cell = lax.broadcasted_iota(jnp.int32, cand.shape, 0)
    w = jnp.where(cell == 0, -jnp.inf, cand)
    for _ in range(PEER_TOPK - 1):
        first = jnp.min(jnp.where(w == _col_max(w), cell, cand.shape[0]), axis=0, keepdims=True)
        w = jnp.where(cell == first, -jnp.inf, w)
    picked = w < cand
    ones = jnp.where(picked, 1.0, 0.0)
    e = jnp.where(picked, jnp.exp(cand - (v1[0:1] + v2[0:1])), 0.0)
    row = lax.broadcasted_iota(jnp.int32, (8, v1.shape[1]), 0)
    low = jnp.zeros((8, v1.shape[1]), F32)
    for a in range(8):
        low = jnp.where(row == a, jnp.sum(ones[starts[a]:starts[a + 1]], axis=0, keepdims=True),
                        low)
    counts = jnp.concatenate([low, ones[starts[8]:starts[8] + 8]], axis=0)
    return counts, 1.0 / jnp.sum(e, axis=0, keepdims=True)


def _peer_kernel(*refs, final_norm):
    if final_norm:
        (h_ref, nw_ref, wqt_ref, k1_ref, k2_ref, pu_ref, pvt_ref, nf_ref, out_ref,
         xnt_scr, cnt_scr, a_scr, rank_scr, b_scr, z_scr, acc_scr) = refs
    else:
        (h_ref, nw_ref, wqt_ref, k1_ref, k2_ref, pu_ref, pvt_ref, out_ref,
         xnt_scr, cnt_scr, a_scr, rank_scr, b_scr, z_scr, acc_scr) = refs
        nf_ref = None
    j = pl.program_id(1)
    tm = h_ref.shape[0]
    nlc = tm // LANES
    nsub = pu_ref.shape[0] // PEER_NKEYS
    dq = 2 * PEER_NKEYS
    quarter = PEER_QUARTER_KEYS * PEER_NKEYS

    @pl.when(j == 0)
    def _prepare():
        xn = _rms(h_ref[...], nw_ref[...])
        xnt = xn.T.astype(BF16)
        xnt_scr[...] = xnt
        qt = jnp.dot(wqt_ref[...], xnt, preferred_element_type=F32).astype(BF16)
        for h in range(PEER_HEADS):
            s1 = jnp.dot(k1_ref[...], qt[h * dq:h * dq + PEER_NKEYS], preferred_element_type=F32)
            s2 = jnp.dot(k2_ref[...], qt[h * dq + PEER_NKEYS:(h + 1) * dq],
                         preferred_element_type=F32)
            for lc in range(nlc):
                cs = slice(lc * LANES, (lc + 1) * LANES)
                s1c = s1[:, cs]
                s2c = s2[:, cs]
                v1, _ = _top_sorted(s1c, False)
                v2, rank2 = _top_sorted(s2c, True)
                counts, zinv = _select_counts(v1, v2)
                last = jnp.min(jnp.where(counts[8:16] > 0.0, v1[8:16], jnp.inf),
                               axis=0, keepdims=True)
                cnt = jnp.where(s1c >= last, jnp.where(s1c < v1[7:8], 1.0, 0.0), 0.0)
                for a in range(8):
                    cnt = jnp.where(s1c == v1[a:a + 1], counts[a:a + 1], cnt)
                cnt_scr[lc, pl.ds(h, PEER_NKEYS, stride=PEER_HEADS), :] = cnt
                a_scr[lc, pl.ds(h, PEER_NKEYS, stride=PEER_HEADS), :] = jnp.exp(s1c - v1[0:1])
                rank_scr[h, :, cs] = rank2
                b_scr[h, :, cs] = jnp.exp(s2c - v2[0:1]) * (0.5 * zinv)
        acc_scr[...] = jnp.zeros_like(acc_scr)

    base = pl.multiple_of(j * nsub, nsub)

    def activations(q):
        return jnp.dot(pu_ref[q * quarter:(q + 1) * quarter, :], xnt_scr[...],
                       preferred_element_type=F32)

    def weigh(act, q):
        for lc in range(nlc):
            cs = slice(lc * LANES, (lc + 1) * LANES)
            keys = [q * PEER_QUARTER_KEYS + ii for ii in range(PEER_QUARTER_KEYS)]
            heads = [pl.ds(pl.multiple_of((base + r) * PEER_HEADS, PEER_HEADS), PEER_HEADS) for r in keys]
            crows = [cnt_scr[lc, hs, :] for hs in heads]
            arows = [a_scr[lc, hs, :] for hs in heads]
            ws = [jnp.zeros((PEER_NKEYS, LANES), F32) for _ in keys]
            for h in range(PEER_HEADS):
                rank = rank_scr[h, :, cs]
                b = b_scr[h, :, cs]
                for n, r in enumerate(keys):
                    hit = rank < crows[n][h:h + 1]
                    ws[n] = ws[n] + jnp.where(hit, b, 0.0) * arows[n][h:h + 1]
            for n, r in enumerate(keys):
                g = _gelu2(act[n * PEER_NKEYS:(n + 1) * PEER_NKEYS, cs])
                z_scr[r * PEER_NKEYS:(r + 1) * PEER_NKEYS, cs] = (ws[n] * g).astype(BF16)

    for q in range(nsub // PEER_QUARTER_KEYS):
        weigh(activations(q), q)
    acc_scr[...] += jnp.dot(pvt_ref[...], z_scr[...], preferred_element_type=F32)

    @pl.when(j == pl.num_programs(1) - 1)
    def _finish():
        y = h_ref[...] + acc_scr[...].T
        if final_norm:
            y = _rms(y, nf_ref[...])
        out_ref[...] = y


def _peer(h, nw, wqt, k1, k2, pu, pvt, nf, layer):
    m, d = h.shape
    ne = pu.shape[1]
    tm = _tile(m, 512)
    eb = PEER_EXPERT_BLOCK
    assert eb == 8 * PEER_NKEYS and ne % eb == 0
    const = lambda i, j: (0, 0)
    in_specs = [pl.BlockSpec((tm, d), lambda i, j: (i, 0)),
                pl.BlockSpec((1, d), const),
                pl.BlockSpec((None,) + wqt.shape[1:], lambda i, j: (layer, 0, 0)),
                pl.BlockSpec(k1.shape, const),
                pl.BlockSpec(k2.shape, const),
                pl.BlockSpec((None, eb, d), lambda i, j: (layer, j, 0)),
                pl.BlockSpec((None, d, eb), lambda i, j: (layer, 0, j))]
    args = [h, nw, wqt, k1, k2, pu, pvt]
    if nf is not None:
        in_specs.append(pl.BlockSpec((1, d), const))
        args.append(nf)
    sel = pltpu.VMEM((PEER_HEADS, PEER_NKEYS, tm), F32)
    sel_kh = pltpu.VMEM((tm // LANES, PEER_NKEYS * PEER_HEADS, LANES), F32)
    return pl.pallas_call(
        functools.partial(_peer_kernel, final_norm=nf is not None),
        grid=(m // tm, ne // eb),
        in_specs=in_specs,
        out_specs=pl.BlockSpec((tm, d), lambda i, j: (i, 0)),
        out_shape=jax.ShapeDtypeStruct((m, d), F32),
        scratch_shapes=[pltpu.VMEM((d, tm), BF16), sel_kh, sel_kh, sel, sel,
                        pltpu.VMEM((eb, tm), BF16),
                        pltpu.VMEM((d, tm), F32)],
        compiler_params=pltpu.CompilerParams(dimension_semantics=("parallel", "arbitrary"),
                                             vmem_limit_bytes=VMEM_LIMIT),
        name="peer",
    )(*args)


def _rope_tables(pos):
    half = HEAD_DIM // 2
    inv = jnp.exp(-math.log(ROPE_THETA) * jnp.arange(half, dtype=F32) / half)
    ang = pos.astype(F32)[:, None] * inv[None, :]
    cos, sin = jnp.cos(ang), jnp.sin(ang)
    return (jnp.tile(jnp.concatenate([cos, cos], axis=-1), (1, RET_HEADS)),
            jnp.tile(jnp.concatenate([-sin, sin], axis=-1), (1, RET_HEADS)))


def _decay_tables(t_idx, same_seq, c_len):
    lg = jnp.log1p(-jnp.exp2(-5.0 - jnp.arange(RET_HEADS, dtype=F32)))[:, None]
    t = t_idx.astype(F32)
    diff = t[:, None] - t[None, :]
    dmat = jnp.where((diff[None] >= 0) & same_seq[None],
                     jnp.exp(jnp.maximum(diff, 0.0)[None] * lg[:, :, None]), 0.0)
    qdec = jnp.repeat(jnp.exp((t + 1.0)[None] * lg).T, HEAD_DIM, axis=1)
    kdec = jnp.repeat(jnp.exp((c_len - 1.0 - t)[None] * lg).T, HEAD_DIM, axis=1)
    sdec = jnp.broadcast_to(jnp.exp(c_len * lg)[:, :, None], (RET_HEADS, HEAD_DIM, HEAD_DIM))
    return dmat, qdec, kdec, sdec


def kernel(x_prompt, x_sample, state_ret, cache_win_k, cache_win_v, norm1_w, norm2_w, normf_w, w_in, b_gate, gm_ln_w, gm_ln_b, gm_ws, gm_b, ret_gn_w, attn_sinks, w_br_a, w_br_b, w_br_c, w_out, peer_wq, peer_k1, peer_k2, peer_u, peer_v):
    batch, seq, d = x_prompt.shape
    ns, t_len, _ = x_sample.shape
    depth = w_in.shape[0]
    wb = cache_win_k.shape[2]
    assert d == D_MODEL and seq % CHUNK == 0 and t_len * SEQ_PER_STEP == CHUNK
    assert ns % SEQ_PER_STEP == 0 and wb == CHUNK
    xp = x_prompt.reshape(batch * seq, d)
    xs = x_sample.reshape(ns * t_len, d)

    row = jnp.arange(CHUNK)
    tabs = {}
    tabs["cos_p"], tabs["sin_p"] = _rope_tables(jnp.arange(seq))
    tabs["cos_s"], tabs["sin_s"] = _rope_tables(PAST_LEN + row % t_len)
    (tabs["dmat_p"], tabs["qdec_p"], tabs["kdec_p"], tabs["sdec_p"]) = _decay_tables(
        row, jnp.ones((CHUNK, CHUNK), bool), float(CHUNK))
    (tabs["dmat_s"], tabs["qdec_s"], tabs["kdec_s"], tabs["sdec_s"]) = _decay_tables(
        row % t_len, (row // t_len)[:, None] == (row // t_len)[None, :], float(t_len))
    causal = jnp.tril(jnp.ones((CHUNK, CHUNK), bool))
    eye = jnp.eye(SEQ_PER_STEP, dtype=F32)

    w_in_all = w_in.astype(BF16)
    wqt_all = jnp.swapaxes(peer_wq, 1, 2).astype(BF16)
    pu_all = peer_u.astype(BF16)
    pvt_all = jnp.swapaxes(peer_v, 1, 2).astype(BF16)

    s_p, s_s, k_p, v_p, k_s, v_s, g_s = [], [], [], [], [], [], []
    for l in range(depth):
        ws = gm_ws[l]
        causal_s = jnp.tril(jnp.ones((t_len, t_len), bool))
        ws_s = jnp.where(causal_s[None], ws[:, :t_len, :t_len], 0.0)
        lw = {
            "wm_p": jnp.where(causal[None], ws, 0.0).astype(BF16),
            "gb_p": jnp.broadcast_to(gm_b[l][:, :, None], (GM_GROUPS, CHUNK, CHUNK)),
            "wm_s": jnp.stack([jnp.kron(eye, ws_s[g]) for g in range(GM_GROUPS)]).astype(BF16),
            "gb_s": jnp.broadcast_to(jnp.tile(gm_b[l][:, :t_len], (1, SEQ_PER_STEP))[:, :, None],
                                     (GM_GROUPS, CHUNK, CHUNK)),
            "ln_w": gm_ln_w[l][None], "ln_b": gm_ln_b[l][None],
            "gn_w": ret_gn_w[l][None], "sinks": attn_sinks[l],
        }
        n1 = norm1_w[l][None]
        proj_p = _in_proj(xp, n1, w_in_all, l, MIX_WIDTH)
        proj_s = _in_proj(xs, n1, w_in_all, l, MIX_WIDTH)
        ocat_p, sp, kp, vp = _prompt_mixer(proj_p, batch, seq, tabs, lw)
        ocat_s, ss, ks, vs, gs = _sample_mixer(
            proj_s, state_ret[l], cache_win_k[l].reshape(ns, wb, 128),
            cache_win_v[l].reshape(ns, wb, 128), tabs, lw)
        gate_w = (n1, w_in_all[l, :, MIX_WIDTH:], b_gate[l][None],
                  jnp.concatenate([w_br_a[l], w_br_b[l], w_br_c[l]], axis=0).astype(BF16),
                  w_out[l].astype(BF16))
        peer_w = (norm2_w[l][None], wqt_all, peer_k1[l].astype(BF16), peer_k2[l].astype(BF16),
                  pu_all, pvt_all, normf_w[None] if l == depth - 1 else None, l)
        xp = _peer(_gate_out(xp, ocat_p, *gate_w), *peer_w)
        xs = _peer(_gate_out(xs, ocat_s, *gate_w), *peer_w)
        s_p.append(sp)
        s_s.append(ss)
        k_p.append(kp.reshape(batch, CHUNK, ATT_KV_HEADS, HEAD_DIM))
        v_p.append(vp.reshape(batch, CHUNK, ATT_KV_HEADS, HEAD_DIM))
        k_s.append(ks.reshape(ns, wb, ATT_KV_HEADS, HEAD_DIM))
        v_s.append(vs.reshape(ns, wb, ATT_KV_HEADS, HEAD_DIM))
        g_s.append(gs.reshape(ns, t_len, GM_WIDTH))
    return (xp.reshape(batch, seq, d), xs.reshape(ns, t_len, d),
            jnp.stack(s_p), jnp.stack(s_s), jnp.stack(k_p), jnp.stack(v_p),
            jnp.stack(k_s), jnp.stack(v_s), jnp.stack(g_s))
```

```python
import functools
import math

import numpy as np
import jax
import jax.numpy as jnp
from jax import lax
from jax.experimental import pallas as pl
from jax.experimental.pallas import tpu as pltpu

F32 = jnp.float32
BF16 = jnp.bfloat16

D_MODEL = 1024
HEAD_DIM = 64
CHUNK = 128
GM_WIDTH = 512
GM_GROUPS = 4
RET_HEADS = 8
ATT_Q_HEADS = 8
ATT_KV_HEADS = 2
ATT_REP = 4
ROPE_THETA = 10000.0
PAST_LEN = 16384
PEER_HEADS = 8
PEER_NKEYS = 128
PEER_TOPK = 16
EPS = 1e-6
NEG = -1e30
MIX_WIDTH = 3840
GATE_WIDTH = 3 * D_MODEL
SEQ_PER_STEP = 16
LANES = 128
PEER_EXPERT_BLOCK = 2048
PEER_GROUP_KEYS = 2
VMEM_LIMIT = 56 * 1024 * 1024


def _tile(m, target):
    t = min(m, target)
    while m % t or t % LANES:
        t -= LANES
    return t


def _rms(x, w):
    return x * lax.rsqrt(jnp.mean(x * x, axis=-1, keepdims=True) + EPS) * w


def _gelu2(x):
    return x * (1.0 + lax.erf(x * np.float32(math.sqrt(0.5))))


def _gelu(x):
    return 0.5 * _gelu2(x)


def _rope(x, cos, sin_signed):
    w = x.shape[-1]
    lane = lax.broadcasted_iota(jnp.int32, x.shape, 1)
    first = (lane & (HEAD_DIM - 1)) < HEAD_DIM // 2
    rot = jnp.where(first, pltpu.roll(x, w - HEAD_DIM // 2, 1), pltpu.roll(x, HEAD_DIM // 2, 1))
    return x * cos + rot * sin_signed


def _head(x, h):
    return x[:, h * HEAD_DIM:(h + 1) * HEAD_DIM]


def _gmlp(gu, gv, lnw, lnb, wm_ref, gb_ref):
    u = _gelu(gu)
    vf = _gelu(gv)
    mu = jnp.mean(vf, axis=-1, keepdims=True)
    var = jnp.mean(jnp.square(vf - mu), axis=-1, keepdims=True)
    vn = (vf - mu) * lax.rsqrt(var + EPS) * lnw + lnb
    vnb = vn.astype(BF16)
    parts = []
    for g in range(GM_GROUPS):
        s = jnp.dot(wm_ref[g], vnb[:, g * LANES:(g + 1) * LANES], preferred_element_type=F32)
        parts.append(s + gb_ref[g])
    return u * jnp.concatenate(parts, axis=-1), vn


def _group_norm(o):
    mu = jnp.mean(o, axis=-1, keepdims=True)
    var = jnp.mean(jnp.square(o - mu), axis=-1, keepdims=True)
    return (o - mu) * lax.rsqrt(var + EPS)


def _dot_nt(a, b):
    return lax.dot_general(a, b, (((1,), (1,)), ((), ())), preferred_element_type=F32)


def _in_proj_kernel(x_ref, nw_ref, w_ref, o_ref):
    xn = _rms(x_ref[...], nw_ref[...]).astype(BF16)
    o_ref[...] = jnp.dot(xn, w_ref[...], preferred_element_type=F32)


def _in_proj(x, nw, w_all, layer, n):
    m, d = x.shape
    tm = _tile(m, 512)
    return pl.pallas_call(
        _in_proj_kernel,
        grid=(m // tm,),
        in_specs=[pl.BlockSpec((tm, d), lambda i: (i, 0)),
                  pl.BlockSpec((1, d), lambda i: (0, 0)),
                  pl.BlockSpec((None, d, n), lambda i: (layer, 0, 0))],
        out_specs=pl.BlockSpec((tm, n), lambda i: (i, 0)),
        out_shape=jax.ShapeDtypeStruct((m, n), F32),
        compiler_params=pltpu.CompilerParams(dimension_semantics=("parallel",),
                                             vmem_limit_bytes=VMEM_LIMIT),
        name="in_proj",
    )(x, nw, w_all)


def _prompt_mixer_kernel(proj_ref, cos_ref, sin_ref, wm_ref, gb_ref, lnw_ref, lnb_ref,
                         dmat_ref, qdec_ref, kdec_ref, sdec_ref, gnw_ref, sinks_ref,
                         ocat_ref, sfin_ref, kout_ref, vout_ref,
                         s_scr, kprev_scr, vprev_scr):
    c = pl.program_id(1)

    @pl.when(c == 0)
    def _reset():
        s_scr[...] = jnp.zeros_like(s_scr)
        kprev_scr[...] = jnp.zeros_like(kprev_scr)
        vprev_scr[...] = jnp.zeros_like(vprev_scr)

    cos = cos_ref[...]
    sin = sin_ref[...]

    o_a, _ = _gmlp(proj_ref[:, 0:512], proj_ref[:, 512:1024], lnw_ref[...], lnb_ref[...],
                   wm_ref, gb_ref)
    ocat_ref[:, 0:512] = o_a.astype(BF16)

    q = _rope(proj_ref[:, 1024:1536], cos, sin)
    ks = _rope(proj_ref[:, 1536:2048], cos, sin) * (HEAD_DIM ** -0.5)
    v = proj_ref[:, 2048:2560]
    qd = q * qdec_ref[...]
    kd = ks * kdec_ref[...]
    ys = []
    for h in range(RET_HEADS):
        qh = _head(q, h).astype(BF16)
        kh = _head(ks, h).astype(BF16)
        vh = _head(v, h).astype(BF16)
        sc = _dot_nt(qh, kh) * dmat_ref[h]
        o = jnp.dot(sc.astype(BF16), vh, preferred_element_type=F32)
        s_old = s_scr[h]
        o = o + jnp.dot(_head(qd, h).astype(BF16), s_old.astype(BF16), preferred_element_type=F32)
        kdh = _head(kd, h).astype(BF16)
        s_new = s_old * sdec_ref[h] + lax.dot_general(
            kdh, vh, (((0,), (0,)), ((), ())), preferred_element_type=F32)
        s_scr[h] = s_new
        sfin_ref[h] = s_new
        ys.append(_group_norm(o))
    y = jnp.concatenate(ys, axis=-1) * gnw_ref[...]
    rg = proj_ref[:, 2560:3072]
    ocat_ref[:, 512:1024] = (rg * jax.nn.sigmoid(rg) * y).astype(BF16)

    qa = _rope(proj_ref[:, 3072:3584], cos, sin)
    ka = _rope(proj_ref[:, 3584:3712], cos[:, 0:128], sin[:, 0:128])
    va = proj_ref[:, 3712:3840]
    kout_ref[...] = ka
    vout_ref[...] = va
    rows = ATT_REP * CHUNK
    tt = lax.broadcasted_iota(jnp.int32, (rows, 2 * CHUNK), 0) & (CHUNK - 1)
    kk = lax.broadcasted_iota(jnp.int32, (rows, 2 * CHUNK), 1)
    valid = (kk > tt) & (kk <= tt + CHUNK) & ((kk >= CHUNK) | (c > 0))
    rr = lax.broadcasted_iota(jnp.int32, (rows, 1), 0) // CHUNK
    outs = [None] * ATT_Q_HEADS
    for g in range(ATT_KV_HEADS):
        kcat = jnp.concatenate([_head(kprev_scr[...], g), _head(ka, g)], axis=0).astype(BF16)
        vcat = jnp.concatenate([_head(vprev_scr[...], g), _head(va, g)], axis=0).astype(BF16)
        qg = jnp.concatenate([_head(qa, g * ATT_REP + r) for r in range(ATT_REP)],
                             axis=0).astype(BF16)
        s = _dot_nt(qg, kcat) * (HEAD_DIM ** -0.5)
        s = jnp.where(valid, s, NEG)
        sink = jnp.zeros((rows, 1), F32)
        for r in range(ATT_REP):
            sink = jnp.where(rr == r, sinks_ref[g * ATT_REP + r], sink)
        mx = jnp.maximum(jnp.max(s, axis=-1, keepdims=True), sink)
        e = jnp.exp(s - mx)
        den = jnp.sum(e, axis=-1, keepdims=True) + jnp.exp(sink - mx)
        p = (e / den).astype(BF16)
        o = jnp.dot(p, vcat, preferred_element_type=F32)
        for r in range(ATT_REP):
            outs[g * ATT_REP + r] = o[r * CHUNK:(r + 1) * CHUNK]
    ocat_ref[:, 1024:1536] = jnp.concatenate(outs, axis=-1).astype(BF16)
    kprev_scr[...] = ka
    vprev_scr[...] = va


def _prompt_mixer(proj, batch, seq, tabs, lw):
    nchunk = seq // CHUNK
    const2 = lambda b, c: (0, 0)
    const3 = lambda b, c: (0, 0, 0)
    row_blk = lambda b, c: (b * nchunk + c, 0)
    return pl.pallas_call(
        _prompt_mixer_kernel,
        grid=(batch, nchunk),
        in_specs=[pl.BlockSpec((CHUNK, MIX_WIDTH), row_blk),
                  pl.BlockSpec((CHUNK, 512), lambda b, c: (c, 0)),
                  pl.BlockSpec((CHUNK, 512), lambda b, c: (c, 0)),
                  pl.BlockSpec((GM_GROUPS, CHUNK, CHUNK), const3),
                  pl.BlockSpec((GM_GROUPS, CHUNK, CHUNK), const3),
                  pl.BlockSpec((1, 512), const2),
                  pl.BlockSpec((1, 512), const2),
                  pl.BlockSpec((RET_HEADS, CHUNK, CHUNK), const3),
                  pl.BlockSpec((CHUNK, 512), const2),
                  pl.BlockSpec((CHUNK, 512), const2),
                  pl.BlockSpec((RET_HEADS, HEAD_DIM, HEAD_DIM), const3),
                  pl.BlockSpec((1, 512), const2),
                  pl.BlockSpec(memory_space=pltpu.SMEM)],
        out_specs=[pl.BlockSpec((CHUNK, 3 * 512), row_blk),
                   pl.BlockSpec((None, RET_HEADS, HEAD_DIM, HEAD_DIM), lambda b, c: (b, 0, 0, 0)),
                   pl.BlockSpec((None, CHUNK, 128), lambda b, c: (b, 0, 0)),
                   pl.BlockSpec((None, CHUNK, 128), lambda b, c: (b, 0, 0))],
        out_shape=[jax.ShapeDtypeStruct((batch * seq, 3 * 512), BF16),
                   jax.ShapeDtypeStruct((batch, RET_HEADS, HEAD_DIM, HEAD_DIM), F32),
                   jax.ShapeDtypeStruct((batch, CHUNK, 128), F32),
                   jax.ShapeDtypeStruct((batch, CHUNK, 128), F32)],
        scratch_shapes=[pltpu.VMEM((RET_HEADS, HEAD_DIM, HEAD_DIM), F32),
                        pltpu.VMEM((CHUNK, 128), F32),
                        pltpu.VMEM((CHUNK, 128), F32)],
        compiler_params=pltpu.CompilerParams(dimension_semantics=("parallel", "arbitrary"),
                                             vmem_limit_bytes=VMEM_LIMIT),
        name="prompt_mixer",
    )(proj, tabs["cos_p"], tabs["sin_p"], lw["wm_p"], lw["gb_p"], lw["ln_w"], lw["ln_b"],
      tabs["dmat_p"], tabs["qdec_p"], tabs["kdec_p"], tabs["sdec_p"], lw["gn_w"], lw["sinks"])


def _sample_mixer_kernel(proj_ref, s0_ref, ck_ref, cv_ref, cos_ref, sin_ref,
                         wm_ref, gb_ref, lnw_ref, lnb_ref, dmat_ref, qdec_ref, kdec_ref,
                         sdec_ref, gnw_ref, sinks_ref,
                         ocat_ref, snew_ref, kout_ref, vout_ref, vn_ref):
    nb = s0_ref.shape[0]
    t_len = proj_ref.shape[0] // nb
    cos = cos_ref[...]
    sin = sin_ref[...]

    o_a, vn = _gmlp(proj_ref[:, 0:512], proj_ref[:, 512:1024], lnw_ref[...], lnb_ref[...],
                    wm_ref, gb_ref)
    ocat_ref[:, 0:512] = o_a.astype(BF16)
    vn_ref[...] = vn

    def seq3(x):
        return x.reshape(nb, t_len, x.shape[-1])

    q = _rope(proj_ref[:, 1024:1536], cos, sin)
    ks = _rope(proj_ref[:, 1536:2048], cos, sin) * (HEAD_DIM ** -0.5)
    v = proj_ref[:, 2048:2560]
    qd = q * qdec_ref[...]
    kd = ks * kdec_ref[...]
    ys = []
    for h in range(RET_HEADS):
        qh = _head(q, h).astype(BF16)
        kh = _head(ks, h).astype(BF16)
        vh = _head(v, h).astype(BF16)
        sc = _dot_nt(qh, kh) * dmat_ref[h]
        o = jnp.dot(sc.astype(BF16), vh, preferred_element_type=F32)
        s_old = s0_ref[:, h]
        oc = jnp.einsum("ntd,nde->nte", seq3(_head(qd, h)).astype(BF16), s_old.astype(BF16),
                        preferred_element_type=F32)
        o = o + oc.reshape(nb * t_len, HEAD_DIM)
        kd3t = jnp.swapaxes(seq3(_head(kd, h)), 1, 2).astype(BF16)
        upd = jnp.einsum("ndt,nte->nde", kd3t, seq3(_head(v, h)).astype(BF16),
                         preferred_element_type=F32)
        snew_ref[:, h] = s_old * sdec_ref[h] + upd
        ys.append(_group_norm(o))
    y = jnp.concatenate(ys, axis=-1) * gnw_ref[...]
    rg = proj_ref[:, 2560:3072]
    ocat_ref[:, 512:1024] = (rg * jax.nn.sigmoid(rg) * y).astype(BF16)

    qa = _rope(proj_ref[:, 3072:3584], cos, sin)
    ka = _rope(proj_ref[:, 3584:3712], cos[:, 0:128], sin[:, 0:128])
    va = proj_ref[:, 3712:3840]
    wb = ck_ref.shape[1]
    kout_ref[:, 0:wb - t_len, :] = ck_ref[:, t_len:wb, :]
    kout_ref[:, wb - t_len:wb, :] = seq3(ka)
    vout_ref[:, 0:wb - t_len, :] = cv_ref[:, t_len:wb, :]
    vout_ref[:, wb - t_len:wb, :] = seq3(va)
    nq = ATT_REP * t_len
    tq_c = lax.broadcasted_iota(jnp.int32, (nb, nq, wb), 1) & (t_len - 1)
    kk_c = lax.broadcasted_iota(jnp.int32, (nb, nq, wb), 2)
    valid_c = kk_c > tq_c
    tq_n = lax.broadcasted_iota(jnp.int32, (nb, nq, t_len), 1) & (t_len - 1)
    kk_n = lax.broadcasted_iota(jnp.int32, (nb, nq, t_len), 2)
    valid_n = kk_n <= tq_n
    rr = lax.broadcasted_iota(jnp.int32, (1, nq, 1), 1) // t_len
    outs = [None] * ATT_Q_HEADS
    for g in range(ATT_KV_HEADS):
        gs = slice(g * HEAD_DIM, (g + 1) * HEAD_DIM)
        kc = ck_ref[:, :, gs].astype(BF16)
        vc = cv_ref[:, :, gs].astype(BF16)
        kn = seq3(_head(ka, g)).astype(BF16)
        vnw = seq3(_head(va, g)).astype(BF16)
        qg = jnp.concatenate([seq3(_head(qa, g * ATT_REP + r)) for r in range(ATT_REP)],
                             axis=1).astype(BF16)
        s_c = jnp.einsum("nqd,nkd->nqk", qg, kc, preferred_element_type=F32) * (HEAD_DIM ** -0.5)
        s_n = jnp.einsum("nqd,nkd->nqk", qg, kn, preferred_element_type=F32) * (HEAD_DIM ** -0.5)
        s_c = jnp.where(valid_c, s_c, NEG)
        s_n = jnp.where(valid_n, s_n, NEG)
        sink = jnp.zeros((1, nq, 1), F32)
        for r in range(ATT_REP):
            sink = jnp.where(rr == r, sinks_ref[g * ATT_REP + r], sink)
        mx = jnp.maximum(jnp.maximum(jnp.max(s_c, axis=-1, keepdims=True),
                                     jnp.max(s_n, axis=-1, keepdims=True)), sink)
        e_c = jnp.exp(s_c - mx)
        e_n = jnp.exp(s_n - mx)
        den = (jnp.sum(e_c, axis=-1, keepdims=True) + jnp.sum(e_n, axis=-1, keepdims=True)
               + jnp.exp(sink - mx))
        o = (jnp.einsum("nqk,nkd->nqd", (e_c / den).astype(BF16), vc, preferred_element_type=F32)
             + jnp.einsum("nqk,nkd->nqd", (e_n / den).astype(BF16), vnw,
                          preferred_element_type=F32))
        for r in range(ATT_REP):
            outs[g * ATT_REP + r] = o[:, r * t_len:(r + 1) * t_len, :].reshape(nb * t_len, HEAD_DIM)
    ocat_ref[:, 1024:1536] = jnp.concatenate(outs, axis=-1).astype(BF16)


def _sample_mixer(proj, s0, ck, cv, tabs, lw):
    ns, wb = ck.shape[0], ck.shape[1]
    nb = SEQ_PER_STEP
    t_len = CHUNK // nb
    const2 = lambda i: (0, 0)
    const3 = lambda i: (0, 0, 0)
    row_blk = lambda i: (i, 0)
    seq_blk3 = lambda i: (i, 0, 0)
    return pl.pallas_call(
        _sample_mixer_kernel,
        grid=(ns // nb,),
        in_specs=[pl.BlockSpec((CHUNK, MIX_WIDTH), row_blk),
                  pl.BlockSpec((nb, RET_HEADS, HEAD_DIM, HEAD_DIM), lambda i: (i, 0, 0, 0)),
                  pl.BlockSpec((nb, wb, 128), seq_blk3),
                  pl.BlockSpec((nb, wb, 128), seq_blk3),
                  pl.BlockSpec((CHUNK, 512), const2),
                  pl.BlockSpec((CHUNK, 512), const2),
                  pl.BlockSpec((GM_GROUPS, CHUNK, CHUNK), const3),
                  pl.BlockSpec((GM_GROUPS, CHUNK, CHUNK), const3),
                  pl.BlockSpec((1, 512), const2),
                  pl.BlockSpec((1, 512), const2),
                  pl.BlockSpec((RET_HEADS, CHUNK, CHUNK), const3),
                  pl.BlockSpec((CHUNK, 512), const2),
                  pl.BlockSpec((CHUNK, 512), const2),
                  pl.BlockSpec((RET_HEADS, HEAD_DIM, HEAD_DIM), const3),
                  pl.BlockSpec((1, 512), const2),
                  pl.BlockSpec(memory_space=pltpu.SMEM)],
        out_specs=[pl.BlockSpec((CHUNK, 3 * 512), row_blk),
                   pl.BlockSpec((nb, RET_HEADS, HEAD_DIM, HEAD_DIM), lambda i: (i, 0, 0, 0)),
                   pl.BlockSpec((nb, wb, 128), seq_blk3),
                   pl.BlockSpec((nb, wb, 128), seq_blk3),
                   pl.BlockSpec((CHUNK, GM_WIDTH), lambda i: (i, 0))],
        out_shape=[jax.ShapeDtypeStruct((ns * t_len, 3 * 512), BF16),
                   jax.ShapeDtypeStruct(s0.shape, F32),
                   jax.ShapeDtypeStruct(ck.shape, F32),
                   jax.ShapeDtypeStruct(cv.shape, F32),
                   jax.ShapeDtypeStruct((ns * t_len, GM_WIDTH), F32)],
        compiler_params=pltpu.CompilerParams(dimension_semantics=("parallel",),
                                             vmem_limit_bytes=VMEM_LIMIT),
        name="sample_mixer",
    )(proj, s0, ck, cv, tabs["cos_s"], tabs["sin_s"], lw["wm_s"], lw["gb_s"],
      lw["ln_w"], lw["ln_b"], tabs["dmat_s"], tabs["qdec_s"], tabs["kdec_s"], tabs["sdec_s"],
      lw["gn_w"], lw["sinks"])


def _gate_out_kernel(x_ref, oc_ref, nw_ref, wg_ref, bg_ref, wbr_ref, wout_ref, h_ref):
    x = x_ref[...]
    xn = _rms(x, nw_ref[...]).astype(BF16)
    m = None
    for b in range(3):
        cols = slice(b * D_MODEL, (b + 1) * D_MODEL)
        gate = jax.nn.sigmoid(jnp.dot(xn, wg_ref[:, cols], preferred_element_type=F32)
                              + bg_ref[:, cols])
        p = jnp.dot(oc_ref[:, b * 512:(b + 1) * 512], wbr_ref[b * 512:(b + 1) * 512, :],
                    preferred_element_type=F32)
        m = gate * p if m is None else m + gate * p
    h_ref[...] = x + jnp.dot(m.astype(BF16), wout_ref[...], preferred_element_type=F32)


def _gate_out(x, ocat, nw, wg, bg, wbr, wout):
    m, d = x.shape
    tm = _tile(m, 512)
    const = lambda i: (0, 0)
    return pl.pallas_call(
        _gate_out_kernel,
        grid=(m // tm,),
        in_specs=[pl.BlockSpec((tm, d), lambda i: (i, 0)),
                  pl.BlockSpec((tm, 3 * 512), lambda i: (i, 0)),
                  pl.BlockSpec((1, d), const),
                  pl.BlockSpec((d, GATE_WIDTH), const),
                  pl.BlockSpec((1, GATE_WIDTH), const),
                  pl.BlockSpec((3 * 512, d), const),
                  pl.BlockSpec((d, d), const)],
        out_specs=pl.BlockSpec((tm, d), lambda i: (i, 0)),
        out_shape=jax.ShapeDtypeStruct((m, d), F32),
        compiler_params=pltpu.CompilerParams(dimension_semantics=("parallel",),
                                             vmem_limit_bytes=VMEM_LIMIT),
        name="gate_out",
    )(x, ocat, nw, wg, bg, wbr, wout)


def _col_max(w):
    return jnp.max(w, axis=0, keepdims=True)


def _top_sorted(w, with_rank):
    row = lax.broadcasted_iota(jnp.int32, (PEER_TOPK, w.shape[1]), 0)
    out = jnp.zeros((PEER_TOPK, w.shape[1]), F32)
    rank = jnp.full(w.shape, float(PEER_TOPK), F32)
    for a in range(PEER_TOPK):
        m = _col_max(w)
        out = jnp.where(row == a, m, out)
        top = w == m
        if with_rank:
            rank = jnp.where(top, float(a), rank)
        if a + 1 < PEER_TOPK:
            w = jnp.where(top, -jnp.inf, w)
    return out, rank


def _select_counts(v1, v2):
    starts, blocks = [0], [v2 + v1[0:1]]
    for a in range(1, 8):
        width = 8 if a < 4 else 4
        starts.append(starts[-1] + blocks[-1].shape[0])
        blocks.append(v2[0:width] + v1[a:a + 1])
    starts.append(starts[-1] + blocks[-1].shape[0])
    blocks.append(v1[8:16] + v2[0:1])
    cand = jnp.concatenate(blocks, axis=0)
    cell = lax.broadcasted_iota(jnp.int32, cand.shape, 0)
    w = jnp.where(cell == 0, -jnp.inf, cand)
    for _ in range(PEER_TOPK - 1):
        first = jnp.min(jnp.where(w == _col_max(w), cell, cand.shape[0]), axis=0, keepdims=True)
        w = jnp.where(cell == first, -jnp.inf, w)
    picked = w < cand
    ones = jnp.where(picked, 1.0, 0.0)
    e = jnp.where(picked, jnp.exp(cand - (v1[0:1] + v2[0:1])), 0.0)
    row = lax.broadcasted_iota(jnp.int32, (8, v1.shape[1]), 0)
    low = jnp.zeros((8, v1.shape[1]), F32)
    for a in range(8):
        low = jnp.where(row == a, jnp.sum(ones[starts[a]:starts[a + 1]], axis=0, keepdims=True),
                        low)
    counts = jnp.concatenate([low, ones[starts[8]:starts[8] + 8]], axis=0)
    return counts, 1.0 / jnp.sum(e, axis=0, keepdims=True)


def _peer_kernel(*refs, final_norm):
    if final_norm:
        (h_ref, nw_ref, wqt_ref, k1_ref, k2_ref, pu_ref, pvt_ref, nf_ref, out_ref,
         xnt_scr, cnt_scr, a_scr, rank_scr, b_scr, z_scr, acc_scr) = refs
    else:
        (h_ref, nw_ref, wqt_ref, k1_ref, k2_ref, pu_ref, pvt_ref, out_ref,
         xnt_scr, cnt_scr, a_scr, rank_scr, b_scr, z_scr, acc_scr) = refs
        nf_ref = None
    j = pl.program_id(1)
    tm = h_ref.shape[0]
    nlc = tm // LANES
    nsub = pu_ref.shape[0] // PEER_NKEYS
    dq = 2 * PEER_NKEYS
    group = PEER_GROUP_KEYS * PEER_NKEYS

    @pl.when(j == 0)
    def _prepare():
        xn = _rms(h_ref[...], nw_ref[...])
        xnt = xn.T.astype(BF16)
        xnt_scr[...] = xnt
        qt = jnp.dot(wqt_ref[...], xnt, preferred_element_type=F32).astype(BF16)
        for h in range(PEER_HEADS):
            s1 = jnp.dot(k1_ref[...], qt[h * dq:h * dq + PEER_NKEYS], preferred_element_type=F32)
            s2 = jnp.dot(k2_ref[...], qt[h * dq + PEER_NKEYS:(h + 1) * dq],
                         preferred_element_type=F32)
            for lc in range(nlc):
                cs = slice(lc * LANES, (lc + 1) * LANES)
                s1c = s1[:, cs]
                s2c = s2[:, cs]
                v1, _ = _top_sorted(s1c, False)
                v2, rank2 = _top_sorted(s2c, True)
                counts, zinv = _select_counts(v1, v2)
                last = jnp.min(jnp.where(counts[8:16] > 0.0, v1[8:16], jnp.inf),
                               axis=0, keepdims=True)
                cnt = jnp.where(s1c >= last, jnp.where(s1c < v1[7:8], 1.0, 0.0), 0.0)
                for a in range(8):
                    cnt = jnp.where(s1c == v1[a:a + 1], counts[a:a + 1], cnt)
                cnt_scr[lc, pl.ds(h, PEER_NKEYS, stride=PEER_HEADS), :] = cnt
                a_scr[lc, pl.ds(h, PEER_NKEYS, stride=PEER_HEADS), :] = jnp.exp(s1c - v1[0:1])
                rank_scr[h, :, cs] = rank2
                b_scr[h, :, cs] = jnp.exp(s2c - v2[0:1]) * (0.5 * zinv)
        acc_scr[...] = jnp.zeros_like(acc_scr)

    base = pl.multiple_of(j * nsub, nsub)

    def activations(q):
        return jnp.dot(pu_ref[q * group:(q + 1) * group, :], xnt_scr[...],
                       preferred_element_type=F32)

    def weigh(act, q):
        for lc in range(nlc):
            cs = slice(lc * LANES, (lc + 1) * LANES)
            keys = [q * PEER_GROUP_KEYS + ii for ii in range(PEER_GROUP_KEYS)]
            heads = [pl.ds(pl.multiple_of((base + r) * PEER_HEADS, PEER_HEADS), PEER_HEADS) for r in keys]
            crows = [cnt_scr[lc, hs, :] for hs in heads]
            arows = [a_scr[lc, hs, :] for hs in heads]
            ws = [jnp.zeros((PEER_NKEYS, LANES), F32) for _ in keys]
            for h in range(PEER_HEADS):
                rank = rank_scr[h, :, cs]
                b = b_scr[h, :, cs]
                for n, r in enumerate(keys):
                    hit = rank < crows[n][h:h + 1]
                    ws[n] = ws[n] + jnp.where(hit, b, 0.0) * arows[n][h:h + 1]
            for n, r in enumerate(keys):
                g = _gelu2(act[n * PEER_NKEYS:(n + 1) * PEER_NKEYS, cs])
                z_scr[r * PEER_NKEYS:(r + 1) * PEER_NKEYS, cs] = (ws[n] * g).astype(BF16)

    for q in range(nsub // PEER_GROUP_KEYS):
        weigh(activations(q), q)
    acc_scr[...] += jnp.dot(pvt_ref[...], z_scr[...], preferred_element_type=F32)

    @pl.when(j == pl.num_programs(1) - 1)
    def _finish():
        y = h_ref[...] + acc_scr[...].T
        if final_norm:
            y = _rms(y, nf_ref[...])
        out_ref[...] = y


def _peer(h, nw, wqt, k1, k2, pu, pvt, nf, layer):
    m, d = h.shape
    ne = pu.shape[1]
    tm = _tile(m, 512)
    eb = PEER_EXPERT_BLOCK
    assert eb % (PEER_GROUP_KEYS * PEER_NKEYS) == 0 and ne % eb == 0
    const = lambda i, j: (0, 0)
    in_specs = [pl.BlockSpec((tm, d), lambda i, j: (i, 0)),
                pl.BlockSpec((1, d), const),
                pl.BlockSpec((None,) + wqt.shape[1:], lambda i, j: (layer, 0, 0)),
                pl.BlockSpec(k1.shape, const),
                pl.BlockSpec(k2.shape, const),
                pl.BlockSpec((None, eb, d), lambda i, j: (layer, j, 0)),
                pl.BlockSpec((None, d, eb), lambda i, j: (layer, 0, j))]
    args = [h, nw, wqt, k1, k2, pu, pvt]
    if nf is not None:
        in_specs.append(pl.BlockSpec((1, d), const))
        args.append(nf)
    sel = pltpu.VMEM((PEER_HEADS, PEER_NKEYS, tm), F32)
    sel_kh = pltpu.VMEM((tm // LANES, PEER_NKEYS * PEER_HEADS, LANES), F32)
    return pl.pallas_call(
        functools.partial(_peer_kernel, final_norm=nf is not None),
        grid=(m // tm, ne // eb),
        in_specs=in_specs,
        out_specs=pl.BlockSpec((tm, d), lambda i, j: (i, 0)),
        out_shape=jax.ShapeDtypeStruct((m, d), F32),
        scratch_shapes=[pltpu.VMEM((d, tm), BF16), sel_kh, sel_kh, sel, sel,
                        pltpu.VMEM((eb, tm), BF16),
                        pltpu.VMEM((d, tm), F32)],
        compiler_params=pltpu.CompilerParams(dimension_semantics=("parallel", "arbitrary"),
                                             vmem_limit_bytes=VMEM_LIMIT),
        name="peer",
    )(*args)


def _rope_tables(pos):
    half = HEAD_DIM // 2
    inv = jnp.exp(-math.log(ROPE_THETA) * jnp.arange(half, dtype=F32) / half)
    ang = pos.astype(F32)[:, None] * inv[None, :]
    cos, sin = jnp.cos(ang), jnp.sin(ang)
    return (jnp.tile(jnp.concatenate([cos, cos], axis=-1), (1, RET_HEADS)),
            jnp.tile(jnp.concatenate([-sin, sin], axis=-1), (1, RET_HEADS)))


def _decay_tables(t_idx, same_seq, c_len):
    lg = jnp.log1p(-jnp.exp2(-5.0 - jnp.arange(RET_HEADS, dtype=F32)))[:, None]
    t = t_idx.astype(F32)
    diff = t[:, None] - t[None, :]
    dmat = jnp.where((diff[None] >= 0) & same_seq[None],
                     jnp.exp(jnp.maximum(diff, 0.0)[None] * lg[:, :, None]), 0.0)
    qdec = jnp.repeat(jnp.exp((t + 1.0)[None] * lg).T, HEAD_DIM, axis=1)
    kdec = jnp.repeat(jnp.exp((c_len - 1.0 - t)[None] * lg).T, HEAD_DIM, axis=1)
    sdec = jnp.broadcast_to(jnp.exp(c_len * lg)[:, :, None], (RET_HEADS, HEAD_DIM, HEAD_DIM))
    return dmat, qdec, kdec, sdec


def kernel(x_prompt, x_sample, state_ret, cache_win_k, cache_win_v, norm1_w, norm2_w, normf_w, w_in, b_gate, gm_ln_w, gm_ln_b, gm_ws, gm_b, ret_gn_w, attn_sinks, w_br_a, w_br_b, w_br_c, w_out, peer_wq, peer_k1, peer_k2, peer_u, peer_v):
    batch, seq, d = x_prompt.shape
    ns, t_len, _ = x_sample.shape
    depth = w_in.shape[0]
    wb = cache_win_k.shape[2]
    assert d == D_MODEL and seq % CHUNK == 0 and t_len * SEQ_PER_STEP == CHUNK
    assert ns % SEQ_PER_STEP == 0 and wb == CHUNK
    xp = x_prompt.reshape(batch * seq, d)
    xs = x_sample.reshape(ns * t_len, d)

    row = jnp.arange(CHUNK)
    tabs = {}
    tabs["cos_p"], tabs["sin_p"] = _rope_tables(jnp.arange(seq))
    tabs["cos_s"], tabs["sin_s"] = _rope_tables(PAST_LEN + row % t_len)
    (tabs["dmat_p"], tabs["qdec_p"], tabs["kdec_p"], tabs["sdec_p"]) = _decay_tables(
        row, jnp.ones((CHUNK, CHUNK), bool), float(CHUNK))
    (tabs["dmat_s"], tabs["qdec_s"], tabs["kdec_s"], tabs["sdec_s"]) = _decay_tables(
        row % t_len, (row // t_len)[:, None] == (row // t_len)[None, :], float(t_len))
    causal = jnp.tril(jnp.ones((CHUNK, CHUNK), bool))
    eye = jnp.eye(SEQ_PER_STEP, dtype=F32)

    w_in_all = w_in.astype(BF16)
    wqt_all = jnp.swapaxes(peer_wq, 1, 2).astype(BF16)
    pu_all = peer_u.astype(BF16)
    pvt_all = jnp.swapaxes(peer_v, 1, 2).astype(BF16)

    s_p, s_s, k_p, v_p, k_s, v_s, g_s = [], [], [], [], [], [], []
    for l in range(depth):
        ws = gm_ws[l]
        causal_s = jnp.tril(jnp.ones((t_len, t_len), bool))
        ws_s = jnp.where(causal_s[None], ws[:, :t_len, :t_len], 0.0)
        lw = {
            "wm_p": jnp.where(causal[None], ws, 0.0).astype(BF16),
            "gb_p": jnp.broadcast_to(gm_b[l][:, :, None], (GM_GROUPS, CHUNK, CHUNK)),
            "wm_s": jnp.stack([jnp.kron(eye, ws_s[g]) for g in range(GM_GROUPS)]).astype(BF16),
            "gb_s": jnp.broadcast_to(jnp.tile(gm_b[l][:, :t_len], (1, SEQ_PER_STEP))[:, :, None],
                                     (GM_GROUPS, CHUNK, CHUNK)),
            "ln_w": gm_ln_w[l][None], "ln_b": gm_ln_b[l][None],
            "gn_w": ret_gn_w[l][None], "sinks": attn_sinks[l],
        }
        n1 = norm1_w[l][None]
        proj_p = _in_proj(xp, n1, w_in_all, l, MIX_WIDTH)
        proj_s = _in_proj(xs, n1, w_in_all, l, MIX_WIDTH)
        ocat_p, sp, kp, vp = _prompt_mixer(proj_p, batch, seq, tabs, lw)
        ocat_s, ss, ks, vs, gs = _sample_mixer(
            proj_s, state_ret[l], cache_win_k[l].reshape(ns, wb, 128),
            cache_win_v[l].reshape(ns, wb, 128), tabs, lw)
        gate_w = (n1, w_in_all[l, :, MIX_WIDTH:], b_gate[l][None],
                  jnp.concatenate([w_br_a[l], w_br_b[l], w_br_c[l]], axis=0).astype(BF16),
                  w_out[l].astype(BF16))
        peer_w = (norm2_w[l][None], wqt_all, peer_k1[l].astype(BF16), peer_k2[l].astype(BF16),
                  pu_all, pvt_all, normf_w[None] if l == depth - 1 else None, l)
        xp = _peer(_gate_out(xp, ocat_p, *gate_w), *peer_w)
        xs = _peer(_gate_out(xs, ocat_s, *gate_w), *peer_w)
        s_p.append(sp)
        s_s.append(ss)
        k_p.append(kp.reshape(batch, CHUNK, ATT_KV_HEADS, HEAD_DIM))
        v_p.append(vp.reshape(batch, CHUNK, ATT_KV_HEADS, HEAD_DIM))
        k_s.append(ks.reshape(ns, wb, ATT_KV_HEADS, HEAD_DIM))
        v_s.append(vs.reshape(ns, wb, ATT_KV_HEADS, HEAD_DIM))
        g_s.append(gs.reshape(ns, t_len, GM_WIDTH))
    return (xp.reshape(batch, seq, d), xs.reshape(ns, t_len, d),
            jnp.stack(s_p), jnp.stack(s_s), jnp.stack(k_p), jnp.stack(v_p),
            jnp.stack(k_s), jnp.stack(v_s), jnp.stack(g_s))
```

```python
import functools
import math

import numpy as np
import jax
import jax.numpy as jnp
from jax import lax
from jax.experimental import pallas as pl
from jax.experimental.pallas import tpu as pltpu

F32 = jnp.float32
BF16 = jnp.bfloat16

D_MODEL = 1024
HEAD_DIM = 64
CHUNK = 128
GM_WIDTH = 512
GM_GROUPS = 4
RET_HEADS = 8
ATT_Q_HEADS = 8
ATT_KV_HEADS = 2
ATT_REP = 4
ROPE_THETA = 10000.0
PAST_LEN = 16384
PEER_HEADS = 8
PEER_NKEYS = 128
PEER_TOPK = 16
EPS = 1e-6
NEG = -1e30
MIX_WIDTH = 3840
GATE_WIDTH = 3 * D_MODEL
SEQ_PER_STEP = 16
LANES = 128
PEER_EXPERT_BLOCK = 2048
PEER_GROUP_KEYS = 2
VMEM_LIMIT = 56 * 1024 * 1024


def _tile(m, target):
    t = min(m, target)
    while m % t or t % LANES:
        t -= LANES
    return t


def _rms(x, w):
    return x * lax.rsqrt(jnp.mean(x * x, axis=-1, keepdims=True) + EPS) * w


def _gelu2(x):
    return x * (1.0 + lax.erf(x * np.float32(math.sqrt(0.5))))


def _gelu(x):
    return 0.5 * _gelu2(x)


def _rope(x, cos, sin_signed):
    w = x.shape[-1]
    lane = lax.broadcasted_iota(jnp.int32, x.shape, 1)
    first = (lane & (HEAD_DIM - 1)) < HEAD_DIM // 2
    rot = jnp.where(first, pltpu.roll(x, w - HEAD_DIM // 2, 1), pltpu.roll(x, HEAD_DIM // 2, 1))
    return x * cos + rot * sin_signed


def _head(x, h):
    return x[:, h * HEAD_DIM:(h + 1) * HEAD_DIM]


def _gmlp(gu, gv, lnw, lnb, wm_ref, gb_ref):
    u = _gelu(gu)
    vf = _gelu(gv)
    mu = jnp.mean(vf, axis=-1, keepdims=True)
    var = jnp.mean(jnp.square(vf - mu), axis=-1, keepdims=True)
    vn = (vf - mu) * lax.rsqrt(var + EPS) * lnw + lnb
    vnb = vn.astype(BF16)
    parts = []
    for g in range(GM_GROUPS):
        s = jnp.dot(wm_ref[g], vnb[:, g * LANES:(g + 1) * LANES], preferred_element_type=F32)
        parts.append(s + gb_ref[g])
    return u * jnp.concatenate(parts, axis=-1), vn


def _group_norm(o):
    mu = jnp.mean(o, axis=-1, keepdims=True)
    var = jnp.mean(jnp.square(o - mu), axis=-1, keepdims=True)
    return (o - mu) * lax.rsqrt(var + EPS)


def _dot_nt(a, b):
    return lax.dot_general(a, b, (((1,), (1,)), ((), ())), preferred_element_type=F32)


def _in_proj_kernel(x_ref, nw_ref, w_ref, o_ref):
    xn = _rms(x_ref[...], nw_ref[...]).astype(BF16)
    o_ref[...] = jnp.dot(xn, w_ref[...], preferred_element_type=F32)


def _in_proj(x, nw, w_all, layer, n):
    m, d = x.shape
    tm = _tile(m, 512)
    return pl.pallas_call(
        _in_proj_kernel,
        grid=(m // tm,),
        in_specs=[pl.BlockSpec((tm, d), lambda i: (i, 0)),
                  pl.BlockSpec((1, d), lambda i: (0, 0)),
                  pl.BlockSpec((None, d, n), lambda i: (layer, 0, 0))],
        out_specs=pl.BlockSpec((tm, n), lambda i: (i, 0)),
        out_shape=jax.ShapeDtypeStruct((m, n), F32),
        compiler_params=pltpu.CompilerParams(dimension_semantics=("parallel",),
                                             vmem_limit_bytes=VMEM_LIMIT),
        name="in_proj",
    )(x, nw, w_all)


def _prompt_mixer_kernel(proj_ref, cos_ref, sin_ref, wm_ref, gb_ref, lnw_ref, lnb_ref,
                         dmat_ref, qdec_ref, kdec_ref, sdec_ref, gnw_ref, sinks_ref,
                         ocat_ref, sfin_ref, kout_ref, vout_ref,
                         s_scr, kprev_scr, vprev_scr):
    c = pl.program_id(1)

    @pl.when(c == 0)
    def _reset():
        s_scr[...] = jnp.zeros_like(s_scr)
        kprev_scr[...] = jnp.zeros_like(kprev_scr)
        vprev_scr[...] = jnp.zeros_like(vprev_scr)

    cos = cos_ref[...]
    sin = sin_ref[...]

    o_a, _ = _gmlp(proj_ref[:, 0:512], proj_ref[:, 512:1024], lnw_ref[...], lnb_ref[...],
                   wm_ref, gb_ref)
    ocat_ref[:, 0:512] = o_a.astype(BF16)

    q = _rope(proj_ref[:, 1024:1536], cos, sin)
    ks = _rope(proj_ref[:, 1536:2048], cos, sin) * (HEAD_DIM ** -0.5)
    v = proj_ref[:, 2048:2560]
    qd = q * qdec_ref[...]
    kd = ks * kdec_ref[...]
    ys = []
    for h in range(RET_HEADS):
        qh = _head(q, h).astype(BF16)
        kh = _head(ks, h).astype(BF16)
        vh = _head(v, h).astype(BF16)
        sc = _dot_nt(qh, kh) * dmat_ref[h]
        o = jnp.dot(sc.astype(BF16), vh, preferred_element_type=F32)
        s_old = s_scr[h]
        o = o + jnp.dot(_head(qd, h).astype(BF16), s_old.astype(BF16), preferred_element_type=F32)
        kdh = _head(kd, h).astype(BF16)
        s_new = s_old * sdec_ref[h] + lax.dot_general(
            kdh, vh, (((0,), (0,)), ((), ())), preferred_element_type=F32)
        s_scr[h] = s_new
        sfin_ref[h] = s_new
        ys.append(_group_norm(o))
    y = jnp.concatenate(ys, axis=-1) * gnw_ref[...]
    rg = proj_ref[:, 2560:3072]
    ocat_ref[:, 512:1024] = (rg * jax.nn.sigmoid(rg) * y).astype(BF16)

    qa = _rope(proj_ref[:, 3072:3584], cos, sin)
    ka = _rope(proj_ref[:, 3584:3712], cos[:, 0:128], sin[:, 0:128])
    va = proj_ref[:, 3712:3840]
    kout_ref[...] = ka
    vout_ref[...] = va
    rows = ATT_REP * CHUNK
    tt = lax.broadcasted_iota(jnp.int32, (rows, 2 * CHUNK), 0) & (CHUNK - 1)
    kk = lax.broadcasted_iota(jnp.int32, (rows, 2 * CHUNK), 1)
    valid = (kk > tt) & (kk <= tt + CHUNK) & ((kk >= CHUNK) | (c > 0))
    rr = lax.broadcasted_iota(jnp.int32, (rows, 1), 0) // CHUNK
    outs = [None] * ATT_Q_HEADS
    for g in range(ATT_KV_HEADS):
        kcat = jnp.concatenate([_head(kprev_scr[...], g), _head(ka, g)], axis=0).astype(BF16)
        vcat = jnp.concatenate([_head(vprev_scr[...], g), _head(va, g)], axis=0).astype(BF16)
        qg = jnp.concatenate([_head(qa, g * ATT_REP + r) for r in range(ATT_REP)],
                             axis=0).astype(BF16)
        s = _dot_nt(qg, kcat) * (HEAD_DIM ** -0.5)
        s = jnp.where(valid, s, NEG)
        sink = jnp.zeros((rows, 1), F32)
        for r in range(ATT_REP):
            sink = jnp.where(rr == r, sinks_ref[g * ATT_REP + r], sink)
        mx = jnp.maximum(jnp.max(s, axis=-1, keepdims=True), sink)
        e = jnp.exp(s - mx)
        den = jnp.sum(e, axis=-1, keepdims=True) + jnp.exp(sink - mx)
        p = (e / den).astype(BF16)
        o = jnp.dot(p, vcat, preferred_element_type=F32)
        for r in range(ATT_REP):
            outs[g * ATT_REP + r] = o[r * CHUNK:(r + 1) * CHUNK]
    ocat_ref[:, 1024:1536] = jnp.concatenate(outs, axis=-1).astype(BF16)
    kprev_scr[...] = ka
    vprev_scr[...] = va


def _prompt_mixer(proj, batch, seq, tabs, lw):
    nchunk = seq // CHUNK
    const2 = lambda b, c: (0, 0)
    const3 = lambda b, c: (0, 0, 0)
    row_blk = lambda b, c: (b * nchunk + c, 0)
    return pl.pallas_call(
        _prompt_mixer_kernel,
        grid=(batch, nchunk),
        in_specs=[pl.BlockSpec((CHUNK, MIX_WIDTH), row_blk),
                  pl.BlockSpec((CHUNK, 512), lambda b, c: (c, 0)),
                  pl.BlockSpec((CHUNK, 512), lambda b, c: (c, 0)),
                  pl.BlockSpec((GM_GROUPS, CHUNK, CHUNK), const3),
                  pl.BlockSpec((GM_GROUPS, CHUNK, CHUNK), const3),
                  pl.BlockSpec((1, 512), const2),
                  pl.BlockSpec((1, 512), const2),
                  pl.BlockSpec((RET_HEADS, CHUNK, CHUNK), const3),
                  pl.BlockSpec((CHUNK, 512), const2),
                  pl.BlockSpec((CHUNK, 512), const2),
                  pl.BlockSpec((RET_HEADS, HEAD_DIM, HEAD_DIM), const3),
                  pl.BlockSpec((1, 512), const2),
                  pl.BlockSpec(memory_space=pltpu.SMEM)],
        out_specs=[pl.BlockSpec((CHUNK, 3 * 512), row_blk),
                   pl.BlockSpec((None, RET_HEADS, HEAD_DIM, HEAD_DIM), lambda b, c: (b, 0, 0, 0)),
                   pl.BlockSpec((None, CHUNK, 128), lambda b, c: (b, 0, 0)),
                   pl.BlockSpec((None, CHUNK, 128), lambda b, c: (b, 0, 0))],
        out_shape=[jax.ShapeDtypeStruct((batch * seq, 3 * 512), BF16),
                   jax.ShapeDtypeStruct((batch, RET_HEADS, HEAD_DIM, HEAD_DIM), F32),
                   jax.ShapeDtypeStruct((batch, CHUNK, 128), F32),
                   jax.ShapeDtypeStruct((batch, CHUNK, 128), F32)],
        scratch_shapes=[pltpu.VMEM((RET_HEADS, HEAD_DIM, HEAD_DIM), F32),
                        pltpu.VMEM((CHUNK, 128), F32),
                        pltpu.VMEM((CHUNK, 128), F32)],
        compiler_params=pltpu.CompilerParams(dimension_semantics=("parallel", "arbitrary"),
                                             vmem_limit_bytes=VMEM_LIMIT),
        name="prompt_mixer",
    )(proj, tabs["cos_p"], tabs["sin_p"], lw["wm_p"], lw["gb_p"], lw["ln_w"], lw["ln_b"],
      tabs["dmat_p"], tabs["qdec_p"], tabs["kdec_p"], tabs["sdec_p"], lw["gn_w"], lw["sinks"])


def _sample_mixer_kernel(proj_ref, s0_ref, ck_ref, cv_ref, cos_ref, sin_ref,
                         wm_ref, gb_ref, lnw_ref, lnb_ref, dmat_ref, qdec_ref, kdec_ref,
                         sdec_ref, gnw_ref, sinks_ref,
                         ocat_ref, snew_ref, kout_ref, vout_ref, vn_ref):
    nb = s0_ref.shape[0]
    t_len = proj_ref.shape[0] // nb
    cos = cos_ref[...]
    sin = sin_ref[...]

    o_a, vn = _gmlp(proj_ref[:, 0:512], proj_ref[:, 512:1024], lnw_ref[...], lnb_ref[...],
                    wm_ref, gb_ref)
    ocat_ref[:, 0:512] = o_a.astype(BF16)
    vn_ref[...] = vn

    def seq3(x):
        return x.reshape(nb, t_len, x.shape[-1])

    q = _rope(proj_ref[:, 1024:1536], cos, sin)
    ks = _rope(proj_ref[:, 1536:2048], cos, sin) * (HEAD_DIM ** -0.5)
    v = proj_ref[:, 2048:2560]
    qd = q * qdec_ref[...]
    kd = ks * kdec_ref[...]
    ys = []
    for h in range(RET_HEADS):
        qh = _head(q, h).astype(BF16)
        kh = _head(ks, h).astype(BF16)
        vh = _head(v, h).astype(BF16)
        sc = _dot_nt(qh, kh) * dmat_ref[h]
        o = jnp.dot(sc.astype(BF16), vh, preferred_element_type=F32)
        s_old = s0_ref[:, h]
        oc = jnp.einsum("ntd,nde->nte", seq3(_head(qd, h)).astype(BF16), s_old.astype(BF16),
                        preferred_element_type=F32)
        o = o + oc.reshape(nb * t_len, HEAD_DIM)
        kd3t = jnp.swapaxes(seq3(_head(kd, h)), 1, 2).astype(BF16)
        upd = jnp.einsum("ndt,nte->nde", kd3t, seq3(_head(v, h)).astype(BF16),
                         preferred_element_type=F32)
        snew_ref[:, h] = s_old * sdec_ref[h] + upd
        ys.append(_group_norm(o))
    y = jnp.concatenate(ys, axis=-1) * gnw_ref[...]
    rg = proj_ref[:, 2560:3072]
    ocat_ref[:, 512:1024] = (rg * jax.nn.sigmoid(rg) * y).astype(BF16)

    qa = _rope(proj_ref[:, 3072:3584], cos, sin)
    ka = _rope(proj_ref[:, 3584:3712], cos[:, 0:128], sin[:, 0:128])
    va = proj_ref[:, 3712:3840]
    wb = ck_ref.shape[1]
    kout_ref[:, 0:wb - t_len, :] = ck_ref[:, t_len:wb, :]
    kout_ref[:, wb - t_len:wb, :] = seq3(ka)
    vout_ref[:, 0:wb - t_len, :] = cv_ref[:, t_len:wb, :]
    vout_ref[:, wb - t_len:wb, :] = seq3(va)
    nq = ATT_REP * t_len
    tq_c = lax.broadcasted_iota(jnp.int32, (nb, nq, wb), 1) & (t_len - 1)
    kk_c = lax.broadcasted_iota(jnp.int32, (nb, nq, wb), 2)
    valid_c = kk_c > tq_c
    tq_n = lax.broadcasted_iota(jnp.int32, (nb, nq, t_len), 1) & (t_len - 1)
    kk_n = lax.broadcasted_iota(jnp.int32, (nb, nq, t_len), 2)
    valid_n = kk_n <= tq_n
    rr = lax.broadcasted_iota(jnp.int32, (1, nq, 1), 1) // t_len
    outs = [None] * ATT_Q_HEADS
    for g in range(ATT_KV_HEADS):
        gs = slice(g * HEAD_DIM, (g + 1) * HEAD_DIM)
        kc = ck_ref[:, :, gs].astype(BF16)
        vc = cv_ref[:, :, gs].astype(BF16)
        kn = seq3(_head(ka, g)).astype(BF16)
        vnw = seq3(_head(va, g)).astype(BF16)
        qg = jnp.concatenate([seq3(_head(qa, g * ATT_REP + r)) for r in range(ATT_REP)],
                             axis=1).astype(BF16)
        s_c = jnp.einsum("nqd,nkd->nqk", qg, kc, preferred_element_type=F32) * (HEAD_DIM ** -0.5)
        s_n = jnp.einsum("nqd,nkd->nqk", qg, kn, preferred_element_type=F32) * (HEAD_DIM ** -0.5)
        s_c = jnp.where(valid_c, s_c, NEG)
        s_n = jnp.where(valid_n, s_n, NEG)
        sink = jnp.zeros((1, nq, 1), F32)
        for r in range(ATT_REP):
            sink = jnp.where(rr == r, sinks_ref[g * ATT_REP + r], sink)
        mx = jnp.maximum(jnp.maximum(jnp.max(s_c, axis=-1, keepdims=True),
                                     jnp.max(s_n, axis=-1, keepdims=True)), sink)
        e_c = jnp.exp(s_c - mx)
        e_n = jnp.exp(s_n - mx)
        den = (jnp.sum(e_c, axis=-1, keepdims=True) + jnp.sum(e_n, axis=-1, keepdims=True)
               + jnp.exp(sink - mx))
        o = (jnp.einsum("nqk,nkd->nqd", (e_c / den).astype(BF16), vc, preferred_element_type=F32)
             + jnp.einsum("nqk,nkd->nqd", (e_n / den).astype(BF16), vnw,
                          preferred_element_type=F32))
        for r in range(ATT_REP):
            outs[g * ATT_REP + r] = o[:, r * t_len:(r + 1) * t_len, :].reshape(nb * t_len, HEAD_DIM)
    ocat_ref[:, 1024:1536] = jnp.concatenate(outs, axis=-1).astype(BF16)


def _sample_mixer(proj, s0, ck, cv, tabs, lw):
    ns, wb = ck.shape[0], ck.shape[1]
    nb = SEQ_PER_STEP
    t_len = CHUNK // nb
    const2 = lambda i: (0, 0)
    const3 = lambda i: (0, 0, 0)
    row_blk = lambda i: (i, 0)
    seq_blk3 = lambda i: (i, 0, 0)
    return pl.pallas_call(
        _sample_mixer_kernel,
        grid=(ns // nb,),
        in_specs=[pl.BlockSpec((CHUNK, MIX_WIDTH), row_blk),
                  pl.BlockSpec((nb, RET_HEADS, HEAD_DIM, HEAD_DIM), lambda i: (i, 0, 0, 0)),
                  pl.BlockSpec((nb, wb, 128), seq_blk3),
                  pl.BlockSpec((nb, wb, 128), seq_blk3),
                  pl.BlockSpec((CHUNK, 512), const2),
                  pl.BlockSpec((CHUNK, 512), const2),
                  pl.BlockSpec((GM_GROUPS, CHUNK, CHUNK), const3),
                  pl.BlockSpec((GM_GROUPS, CHUNK, CHUNK), const3),
                  pl.BlockSpec((1, 512), const2),
                  pl.BlockSpec((1, 512), const2),
                  pl.BlockSpec((RET_HEADS, CHUNK, CHUNK), const3),
                  pl.BlockSpec((CHUNK, 512), const2),
                  pl.BlockSpec((CHUNK, 512), const2),
                  pl.BlockSpec((RET_HEADS, HEAD_DIM, HEAD_DIM), const3),
                  pl.BlockSpec((1, 512), const2),
                  pl.BlockSpec(memory_space=pltpu.SMEM)],
        out_specs=[pl.BlockSpec((CHUNK, 3 * 512), row_blk),
                   pl.BlockSpec((nb, RET_HEADS, HEAD_DIM, HEAD_DIM), lambda i: (i, 0, 0, 0)),
                   pl.BlockSpec((nb, wb, 128), seq_blk3),
                   pl.BlockSpec((nb, wb, 128), seq_blk3),
                   pl.BlockSpec((CHUNK, GM_WIDTH), lambda i: (i, 0))],
        out_shape=[jax.ShapeDtypeStruct((ns * t_len, 3 * 512), BF16),
                   jax.ShapeDtypeStruct(s0.shape, F32),
                   jax.ShapeDtypeStruct(ck.shape, F32),
                   jax.ShapeDtypeStruct(cv.shape, F32),
                   jax.ShapeDtypeStruct((ns * t_len, GM_WIDTH), F32)],
        compiler_params=pltpu.CompilerParams(dimension_semantics=("parallel",),
                                             vmem_limit_bytes=VMEM_LIMIT),
        name="sample_mixer",
    )(proj, s0, ck, cv, tabs["cos_s"], tabs["sin_s"], lw["wm_s"], lw["gb_s"],
      lw["ln_w"], lw["ln_b"], tabs["dmat_s"], tabs["qdec_s"], tabs["kdec_s"], tabs["sdec_s"],
      lw["gn_w"], lw["sinks"])


def _gate_out_kernel(x_ref, oc_ref, nw_ref, wg_ref, bg_ref, wbr_ref, wout_ref, h_ref):
    x = x_ref[...]
    xn = _rms(x, nw_ref[...]).astype(BF16)
    m = None
    for b in range(3):
        cols = slice(b * D_MODEL, (b + 1) * D_MODEL)
        gate = jax.nn.sigmoid(jnp.dot(xn, wg_ref[:, cols], preferred_element_type=F32)
                              + bg_ref[:, cols])
        p = jnp.dot(oc_ref[:, b * 512:(b + 1) * 512], wbr_ref[b * 512:(b + 1) * 512, :],
                    preferred_element_type=F32)
        m = gate * p if m is None else m + gate * p
    h_ref[...] = x + jnp.dot(m.astype(BF16), wout_ref[...], preferred_element_type=F32)


def _gate_out(x, ocat, nw, wg, bg, wbr, wout):
    m, d = x.shape
    tm = _tile(m, 512)
    const = lambda i: (0, 0)
    return pl.pallas_call(
        _gate_out_kernel,
        grid=(m // tm,),
        in_specs=[pl.BlockSpec((tm, d), lambda i: (i, 0)),
                  pl.BlockSpec((tm, 3 * 512), lambda i: (i, 0)),
                  pl.BlockSpec((1, d), const),
                  pl.BlockSpec((d, GATE_WIDTH), const),
                  pl.BlockSpec((1, GATE_WIDTH), const),
                  pl.BlockSpec((3 * 512, d), const),
                  pl.BlockSpec((d, d), const)],
        out_specs=pl.BlockSpec((tm, d), lambda i: (i, 0)),
        out_shape=jax.ShapeDtypeStruct((m, d), F32),
        compiler_params=pltpu.CompilerParams(dimension_semantics=("parallel",),
                                             vmem_limit_bytes=VMEM_LIMIT),
        name="gate_out",
    )(x, ocat, nw, wg, bg, wbr, wout)


def _col_max(w):
    return jnp.max(w, axis=0, keepdims=True)


def _top_sorted(w, with_rank):
    row = lax.broadcasted_iota(jnp.int32, (PEER_TOPK, w.shape[1]), 0)
    out = jnp.zeros((PEER_TOPK, w.shape[1]), F32)
    rank = jnp.full(w.shape, float(PEER_TOPK), F32)
    for a in range(PEER_TOPK):
        m = _col_max(w)
        out = jnp.where(row == a, m, out)
        top = w == m
        if with_rank:
            rank = jnp.where(top, float(a), rank)
        if a + 1 < PEER_TOPK:
            w = jnp.where(top, -jnp.inf, w)
    return out, rank


def _select_counts(v1, v2):
    starts, blocks = [0], [v2 + v1[0:1]]
    for a in range(1, 8):
        width = 8 if a < 4 else 4
        starts.append(starts[-1] + blocks[-1].shape[0])
        blocks.append(v2[0:width] + v1[a:a + 1])
    starts.append(starts[-1] + blocks[-1].shape[0])
    blocks.append(v1[8:16] + v2[0:1])
    cand = jnp.concatenate(blocks, axis=0)
    cell = lax.broadcasted_iota(jnp.int32, cand.shape, 0)
    w = jnp.where(cell == 0, -jnp.inf, cand)
    for _ in range(PEER_TOPK - 1):
        first = jnp.min(jnp.where(w == _col_max(w), cell, cand.shape[0]), axis=0, keepdims=True)
        w = jnp.where(cell == first, -jnp.inf, w)
    picked = w < cand
    ones = jnp.where(picked, 1.0, 0.0)
    e = jnp.where(picked, jnp.exp(cand - (v1[0:1] + v2[0:1])), 0.0)
    row = lax.broadcasted_iota(jnp.int32, (8, v1.shape[1]), 0)
    low = jnp.zeros((8, v1.shape[1]), F32)
    for a in range(8):
        low = jnp.where(row == a, jnp.sum(ones[starts[a]:starts[a + 1]], axis=0, keepdims=True),
                        low)
    counts = jnp.concatenate([low, ones[starts[8]:starts[8] + 8]], axis=0)
    return counts, 1.0 / jnp.sum(e, axis=0, keepdims=True)


def _peer_select_kernel(h_ref, nw_ref, wqt_ref, k1_ref, k2_ref,
                        xnt_scr, cnt_scr, a_scr, rank_scr, b_scr):
    tm = h_ref.shape[0]
    nlc = tm // LANES
    dq = 2 * PEER_NKEYS

    def _prepare():
        xn = _rms(h_ref[...], nw_ref[...])
        xnt = xn.T.astype(BF16)
        xnt_scr[...] = xnt
        qt = jnp.dot(wqt_ref[...], xnt, preferred_element_type=F32).astype(BF16)
        for h in range(PEER_HEADS):
            s1 = jnp.dot(k1_ref[...], qt[h * dq:h * dq + PEER_NKEYS], preferred_element_type=F32)
            s2 = jnp.dot(k2_ref[...], qt[h * dq + PEER_NKEYS:(h + 1) * dq],
                         preferred_element_type=F32)
            for lc in range(nlc):
                cs = slice(lc * LANES, (lc + 1) * LANES)
                s1c = s1[:, cs]
                s2c = s2[:, cs]
                v1, _ = _top_sorted(s1c, False)
                v2, rank2 = _top_sorted(s2c, True)
                counts, zinv = _select_counts(v1, v2)
                last = jnp.min(jnp.where(counts[8:16] > 0.0, v1[8:16], jnp.inf),
                               axis=0, keepdims=True)
                cnt = jnp.where(s1c >= last, jnp.where(s1c < v1[7:8], 1.0, 0.0), 0.0)
                for a in range(8):
                    cnt = jnp.where(s1c == v1[a:a + 1], counts[a:a + 1], cnt)
                cnt_scr[lc, pl.ds(h, PEER_NKEYS, stride=PEER_HEADS), :] = cnt
                a_scr[lc, pl.ds(h, PEER_NKEYS, stride=PEER_HEADS), :] = jnp.exp(s1c - v1[0:1])
                rank_scr[h, :, cs] = rank2
                b_scr[h, :, cs] = jnp.exp(s2c - v2[0:1]) * (0.5 * zinv)

    _prepare()


def _peer_dense_kernel(*refs, final_norm):
    if final_norm:
        (h_ref, xnt_scr, cnt_scr, a_scr, rank_scr, b_scr, pu_ref, pvt_ref, nf_ref, out_ref,
         z_scr, acc_scr) = refs
    else:
        (h_ref, xnt_scr, cnt_scr, a_scr, rank_scr, b_scr, pu_ref, pvt_ref, out_ref,
         z_scr, acc_scr) = refs
        nf_ref = None
    j = pl.program_id(1)
    tm = h_ref.shape[0]
    nlc = tm // LANES
    nsub = pu_ref.shape[0] // PEER_NKEYS
    group = PEER_GROUP_KEYS * PEER_NKEYS

    @pl.when(j == 0)
    def _reset():
        acc_scr[...] = jnp.zeros_like(acc_scr)

    base = pl.multiple_of(j * nsub, nsub)

    def activations(q):
        return jnp.dot(pu_ref[q * group:(q + 1) * group, :], xnt_scr[...],
                       preferred_element_type=F32)

    def weigh(act, q):
        for lc in range(nlc):
            cs = slice(lc * LANES, (lc + 1) * LANES)
            keys = [q * PEER_GROUP_KEYS + ii for ii in range(PEER_GROUP_KEYS)]
            heads = [pl.ds(pl.multiple_of((base + r) * PEER_HEADS, PEER_HEADS), PEER_HEADS) for r in keys]
            crows = [cnt_scr[lc, hs, :] for hs in heads]
            arows = [a_scr[lc, hs, :] for hs in heads]
            ws = [jnp.zeros((PEER_NKEYS, LANES), F32) for _ in keys]
            for h in range(PEER_HEADS):
                rank = rank_scr[h, :, cs]
                b = b_scr[h, :, cs]
                for n, r in enumerate(keys):
                    hit = rank < crows[n][h:h + 1]
                    ws[n] = ws[n] + jnp.where(hit, b, 0.0) * arows[n][h:h + 1]
            for n, r in enumerate(keys):
                g = _gelu2(act[n * PEER_NKEYS:(n + 1) * PEER_NKEYS, cs])
                z_scr[r * PEER_NKEYS:(r + 1) * PEER_NKEYS, cs] = (ws[n] * g).astype(BF16)

    for q in range(nsub // PEER_GROUP_KEYS):
        weigh(activations(q), q)
    acc_scr[...] += jnp.dot(pvt_ref[...], z_scr[...], preferred_element_type=F32)

    @pl.when(j == pl.num_programs(1) - 1)
    def _finish():
        y = h_ref[...] + acc_scr[...].T
        if final_norm:
            y = _rms(y, nf_ref[...])
        out_ref[...] = y


def _peer(h, nw, wqt, k1, k2, pu, pvt, nf, layer):
    m, d = h.shape
    ne = pu.shape[1]
    tm = _tile(m, 512)
    nlc = tm // LANES
    eb = PEER_EXPERT_BLOCK
    assert eb % (PEER_GROUP_KEYS * PEER_NKEYS) == 0 and ne % eb == 0
    kh_rows = PEER_NKEYS * PEER_HEADS
    table_shapes = [jax.ShapeDtypeStruct((d, m), BF16),
                    jax.ShapeDtypeStruct((m // LANES, kh_rows, LANES), F32),
                    jax.ShapeDtypeStruct((m // LANES, kh_rows, LANES), F32),
                    jax.ShapeDtypeStruct((PEER_HEADS, PEER_NKEYS, m), F32),
                    jax.ShapeDtypeStruct((PEER_HEADS, PEER_NKEYS, m), F32)]
    table_specs = [pl.BlockSpec((d, tm), lambda i, *_: (0, i)),
                   pl.BlockSpec((nlc, kh_rows, LANES), lambda i, *_: (i, 0, 0)),
                   pl.BlockSpec((nlc, kh_rows, LANES), lambda i, *_: (i, 0, 0)),
                   pl.BlockSpec((PEER_HEADS, PEER_NKEYS, tm), lambda i, *_: (0, 0, i)),
                   pl.BlockSpec((PEER_HEADS, PEER_NKEYS, tm), lambda i, *_: (0, 0, i))]
    tables = pl.pallas_call(
        _peer_select_kernel,
        grid=(m // tm,),
        in_specs=[pl.BlockSpec((tm, d), lambda i: (i, 0)),
                  pl.BlockSpec((1, d), lambda i: (0, 0)),
                  pl.BlockSpec((None,) + wqt.shape[1:], lambda i: (layer, 0, 0)),
                  pl.BlockSpec(k1.shape, lambda i: (0, 0)),
                  pl.BlockSpec(k2.shape, lambda i: (0, 0))],
        out_specs=table_specs,
        out_shape=table_shapes,
        compiler_params=pltpu.CompilerParams(dimension_semantics=("parallel",),
                                             vmem_limit_bytes=VMEM_LIMIT),
        name="peer_select",
    )(h, nw, wqt, k1, k2)

    in_specs = ([pl.BlockSpec((tm, d), lambda i, j: (i, 0))] + table_specs
                + [pl.BlockSpec((None, eb, d), lambda i, j: (layer, j, 0)),
                   pl.BlockSpec((None, d, eb), lambda i, j: (layer, 0, j))])
    args = [h, *tables, pu, pvt]
    if nf is not None:
        in_specs.append(pl.BlockSpec((1, d), lambda i, j: (0, 0)))
        args.append(nf)
    return pl.pallas_call(
        functools.partial(_peer_dense_kernel, final_norm=nf is not None),
        grid=(m // tm, ne // eb),
        in_specs=in_specs,
        out_specs=pl.BlockSpec((tm, d), lambda i, j: (i, 0)),
        out_shape=jax.ShapeDtypeStruct((m, d), F32),
        scratch_shapes=[pltpu.VMEM((eb, tm), BF16), pltpu.VMEM((d, tm), F32)],
        compiler_params=pltpu.CompilerParams(dimension_semantics=("parallel", "arbitrary"),
                                             vmem_limit_bytes=VMEM_LIMIT),
        name="peer",
    )(*args)


def _rope_tables(pos):
    half = HEAD_DIM // 2
    inv = jnp.exp(-math.log(ROPE_THETA) * jnp.arange(half, dtype=F32) / half)
    ang = pos.astype(F32)[:, None] * inv[None, :]
    cos, sin = jnp.cos(ang), jnp.sin(ang)
    return (jnp.tile(jnp.concatenate([cos, cos], axis=-1), (1, RET_HEADS)),
            jnp.tile(jnp.concatenate([-sin, sin], axis=-1), (1, RET_HEADS)))


def _decay_tables(t_idx, same_seq, c_len):
    lg = jnp.log1p(-jnp.exp2(-5.0 - jnp.arange(RET_HEADS, dtype=F32)))[:, None]
    t = t_idx.astype(F32)
    diff = t[:, None] - t[None, :]
    dmat = jnp.where((diff[None] >= 0) & same_seq[None],
                     jnp.exp(jnp.maximum(diff, 0.0)[None] * lg[:, :, None]), 0.0)
    qdec = jnp.repeat(jnp.exp((t + 1.0)[None] * lg).T, HEAD_DIM, axis=1)
    kdec = jnp.repeat(jnp.exp((c_len - 1.0 - t)[None] * lg).T, HEAD_DIM, axis=1)
    sdec = jnp.broadcast_to(jnp.exp(c_len * lg)[:, :, None], (RET_HEADS, HEAD_DIM, HEAD_DIM))
    return dmat, qdec, kdec, sdec


def kernel(x_prompt, x_sample, state_ret, cache_win_k, cache_win_v, norm1_w, norm2_w, normf_w, w_in, b_gate, gm_ln_w, gm_ln_b, gm_ws, gm_b, ret_gn_w, attn_sinks, w_br_a, w_br_b, w_br_c, w_out, peer_wq, peer_k1, peer_k2, peer_u, peer_v):
    batch, seq, d = x_prompt.shape
    ns, t_len, _ = x_sample.shape
    depth = w_in.shape[0]
    wb = cache_win_k.shape[2]
    assert d == D_MODEL and seq % CHUNK == 0 and t_len * SEQ_PER_STEP == CHUNK
    assert ns % SEQ_PER_STEP == 0 and wb == CHUNK
    xp = x_prompt.reshape(batch * seq, d)
    xs = x_sample.reshape(ns * t_len, d)

    row = jnp.arange(CHUNK)
    tabs = {}
    tabs["cos_p"], tabs["sin_p"] = _rope_tables(jnp.arange(seq))
    tabs["cos_s"], tabs["sin_s"] = _rope_tables(PAST_LEN + row % t_len)
    (tabs["dmat_p"], tabs["qdec_p"], tabs["kdec_p"], tabs["sdec_p"]) = _decay_tables(
        row, jnp.ones((CHUNK, CHUNK), bool), float(CHUNK))
    (tabs["dmat_s"], tabs["qdec_s"], tabs["kdec_s"], tabs["sdec_s"]) = _decay_tables(
        row % t_len, (row // t_len)[:, None] == (row // t_len)[None, :], float(t_len))
    causal = jnp.tril(jnp.ones((CHUNK, CHUNK), bool))
    eye = jnp.eye(SEQ_PER_STEP, dtype=F32)

    w_in_all = w_in.astype(BF16)
    wqt_all = jnp.swapaxes(peer_wq, 1, 2).astype(BF16)
    pu_all = peer_u.astype(BF16)
    pvt_all = jnp.swapaxes(peer_v, 1, 2).astype(BF16)

    s_p, s_s, k_p, v_p, k_s, v_s, g_s = [], [], [], [], [], [], []
    for l in range(depth):
        ws = gm_ws[l]
        causal_s = jnp.tril(jnp.ones((t_len, t_len), bool))
        ws_s = jnp.where(causal_s[None], ws[:, :t_len, :t_len], 0.0)
        lw = {
            "wm_p": jnp.where(causal[None], ws, 0.0).astype(BF16),
            "gb_p": jnp.broadcast_to(gm_b[l][:, :, None], (GM_GROUPS, CHUNK, CHUNK)),
            "wm_s": jnp.stack([jnp.kron(eye, ws_s[g]) for g in range(GM_GROUPS)]).astype(BF16),
            "gb_s": jnp.broadcast_to(jnp.tile(gm_b[l][:, :t_len], (1, SEQ_PER_STEP))[:, :, None],
                                     (GM_GROUPS, CHUNK, CHUNK)),
            "ln_w": gm_ln_w[l][None], "ln_b": gm_ln_b[l][None],
            "gn_w": ret_gn_w[l][None], "sinks": attn_sinks[l],
        }
        n1 = norm1_w[l][None]
        proj_p = _in_proj(xp, n1, w_in_all, l, MIX_WIDTH)
        proj_s = _in_proj(xs, n1, w_in_all, l, MIX_WIDTH)
        ocat_p, sp, kp, vp = _prompt_mixer(proj_p, batch, seq, tabs, lw)
        ocat_s, ss, ks, vs, gs = _sample_mixer(
            proj_s, state_ret[l], cache_win_k[l].reshape(ns, wb, 128),
            cache_win_v[l].reshape(ns, wb, 128), tabs, lw)
        gate_w = (n1, w_in_all[l, :, MIX_WIDTH:], b_gate[l][None],
                  jnp.concatenate([w_br_a[l], w_br_b[l], w_br_c[l]], axis=0).astype(BF16),
                  w_out[l].astype(BF16))
        peer_w = (norm2_w[l][None], wqt_all, peer_k1[l].astype(BF16), peer_k2[l].astype(BF16),
                  pu_all, pvt_all, normf_w[None] if l == depth - 1 else None, l)
        xp = _peer(_gate_out(xp, ocat_p, *gate_w), *peer_w)
        xs = _peer(_gate_out(xs, ocat_s, *gate_w), *peer_w)
        s_p.append(sp)
        s_s.append(ss)
        k_p.append(kp.reshape(batch, CHUNK, ATT_KV_HEADS, HEAD_DIM))
        v_p.append(vp.reshape(batch, CHUNK, ATT_KV_HEADS, HEAD_DIM))
        k_s.append(ks.reshape(ns, wb, ATT_KV_HEADS, HEAD_DIM))
        v_s.append(vs.reshape(ns, wb, ATT_KV_HEADS, HEAD_DIM))
        g_s.append(gs.reshape(ns, t_len, GM_WIDTH))
    return (xp.reshape(batch, seq, d), xs.reshape(ns, t_len, d),
            jnp.stack(s_p), jnp.stack(s_s), jnp.stack(k_p), jnp.stack(v_p),
            jnp.stack(k_s), jnp.stack(v_s), jnp.stack(g_s))
```
